```python
import math
import jax, jax.numpy as jnp
from jax import lax
import numpy as np

D_MODEL = 1024
BATCH = 16
SEQ = 2048
DEPTH = 2

GROUP_WIDTH = D_MODEL // 4
HEAD_DIM = 64
N_HEADS_GROUP = GROUP_WIDTH // HEAD_DIM
D_MIX = 4 * GROUP_WIDTH

MOBA_BLOCK = 256
MOBA_TOPK = 3
MOBA_Q_CHUNK = 32
MLA_Q_RANK = 192
MLA_KV_RANK = 128
MLA_NOPE = 64
MLA_ROPE = 32
MLA_V = HEAD_DIM
MLA_QK = MLA_NOPE + MLA_ROPE
SB_Q_BLOCK = 128
SSM_HEADDIM = 64
SSM_HEADS = GROUP_WIDTH // SSM_HEADDIM
SSM_GROUPS = 2
SSM_STATE = 64
SSM_CONV = 4
SSM_CHUNK = 128
SSM_XBC = GROUP_WIDTH + 2 * SSM_GROUPS * SSM_STATE
D_FF = 2816
FFN_CONV = 3
ROPE_THETA = 10000.0
ATTN_Q_BLOCK = 128
EPS = 1e-6

MOBA_COLS = 3 * GROUP_WIDTH
MLA_COLS = MLA_Q_RANK + MLA_KV_RANK + MLA_ROPE
SB_COLS = 3 * GROUP_WIDTH
SSM_COLS = GROUP_WIDTH + SSM_XBC + SSM_HEADS
IN_COLS = MOBA_COLS + MLA_COLS + SB_COLS + SSM_COLS

kernel_name = "hymba_moba_mla_sb_ssd_convffn"


def _split(x, sizes):
    outs, o = [], 0
    for s in sizes:
        outs.append(x[..., o:o + s])
        o += s
    return outs


def rms_norm(x, g):
    xf = x.astype(jnp.float32)
    y = xf * lax.rsqrt(jnp.mean(xf * xf, axis=-1, keepdims=True) + EPS)
    return (y * g.astype(jnp.float32)).astype(x.dtype)


def rope_tables(positions, dim):
    inv = 1.0 / (ROPE_THETA ** (jnp.arange(0, dim, 2, dtype=jnp.float32) / dim))
    ang = positions.astype(jnp.float32)[..., None] * inv
    return jnp.cos(ang), jnp.sin(ang)


def apply_rope(x, cos, sin):
    x1, x2 = jnp.split(x, 2, axis=-1)
    c = cos[:, :, None, :].astype(x.dtype)
    s = sin[:, :, None, :].astype(x.dtype)
    return jnp.concatenate([x1 * c - x2 * s, x1 * s + x2 * c], axis=-1)


def causal_dwconv(x, w):
    width, ch = w.shape
    return lax.conv_general_dilated(
        x, w[:, None, :], window_strides=(1,), padding=[(width - 1, 0)],
        dimension_numbers=("NWC", "WIO", "NWC"), feature_group_count=ch)


def moba_attention(q, k, v):
    bsz, seq, nh, hd = q.shape
    scale = hd ** -0.5
    nblk = -(-seq // MOBA_BLOCK)
    pad = nblk * MOBA_BLOCK - seq
    kp = jnp.pad(k, ((0, 0), (0, pad), (0, 0), (0, 0)))
    vp = jnp.pad(v, ((0, 0), (0, pad), (0, 0), (0, 0)))
    kbt = kp.reshape(bsz, nblk, MOBA_BLOCK, nh, hd).transpose(0, 3, 1, 2, 4)
    vbt = vp.reshape(bsz, nblk, MOBA_BLOCK, nh, hd).transpose(0, 3, 1, 2, 4)
    kmean = jnp.mean(kbt.astype(jnp.float32), axis=3).astype(k.dtype)
    k_eff = min(MOBA_TOPK, nblk)
    nq = seq // MOBA_Q_CHUNK
    qc = q.reshape(bsz, nq, MOBA_Q_CHUNK, nh, hd).transpose(1, 0, 2, 3, 4)
    blk_ids = jnp.arange(nblk)
    bi = jnp.arange(bsz)[:, None, None, None]
    hi = jnp.arange(nh)[None, :, None, None]

    def one_chunk(args):
        c, qi = args
        cur = (c * MOBA_Q_CHUNK) // MOBA_BLOCK
        gate = jnp.einsum("bqhd,bhnd->bhqn", qi, kmean).astype(jnp.float32)
        gate = jnp.where(blk_ids < cur, gate, -jnp.inf)
        _, idx = lax.top_k(gate, k_eff)
        valid = jnp.arange(k_eff) < cur
        k_sel = kbt[bi, hi, idx]
        v_sel = vbt[bi, hi, idx]
        s_sel = jnp.einsum("bqhd,bhqnjd->bhqnj", qi, k_sel).astype(jnp.float32) * scale
        s_sel = jnp.where(valid[None, None, None, :, None], s_sel, -jnp.inf)
        s_sel = s_sel.reshape(bsz, nh, MOBA_Q_CHUNK, k_eff * MOBA_BLOCK)
        k_own = lax.dynamic_index_in_dim(kbt, cur, axis=2, keepdims=False)
        v_own = lax.dynamic_index_in_dim(vbt, cur, axis=2, keepdims=False)
        s_own = jnp.einsum("bqhd,bhjd->bhqj", qi, k_own).astype(jnp.float32) * scale
        q_local = c * MOBA_Q_CHUNK + jnp.arange(MOBA_Q_CHUNK) - cur * MOBA_BLOCK
        causal = jnp.arange(MOBA_BLOCK)[None, :] <= q_local[:, None]
        s_own = jnp.where(causal, s_own, -jnp.inf)
        p = jax.nn.softmax(jnp.concatenate([s_sel, s_own], axis=-1), axis=-1).astype(v.dtype)
        p_sel = p[..., :k_eff * MOBA_BLOCK].reshape(bsz, nh, MOBA_Q_CHUNK, k_eff, MOBA_BLOCK)
        p_own = p[..., k_eff * MOBA_BLOCK:]
        return (jnp.einsum("bhqnj,bhqnjd->bqhd", p_sel, v_sel)
                + jnp.einsum("bhqj,bhjd->bqhd", p_own, v_own))

    out = lax.map(one_chunk, (jnp.arange(nq), qc))
    return out.transpose(1, 0, 2, 3, 4).reshape(bsz, seq, nh, hd)


def moba_mixer(p, cos, sin, qk_g):
    bsz, seq, _ = p.shape
    q, k, v = [t.reshape(bsz, seq, N_HEADS_GROUP, HEAD_DIM) for t in _split(p, (GROUP_WIDTH,) * 3)]
    q = apply_rope(rms_norm(q, qk_g[0]), cos, sin)
    k = apply_rope(rms_norm(k, qk_g[1]), cos, sin)
    return moba_attention(q, k, v)


def causal_softmax_attention(q, k, v, scale):
    bsz, seq, nh, dk = q.shape
    nb = seq // ATTN_Q_BLOCK
    qb = q.reshape(bsz, nb, ATTN_Q_BLOCK, nh, dk).transpose(1, 0, 2, 3, 4)
    kpos = jnp.arange(seq)

    def one_block(args):
        i, qi = args
        s = jnp.einsum("bqhd,bkhd->bhqk", qi, k).astype(jnp.float32) * scale
        qpos = i * ATTN_Q_BLOCK + jnp.arange(ATTN_Q_BLOCK)
        s = jnp.where(kpos[None, :] <= qpos[:, None], s, -jnp.inf)
        pr = jax.nn.softmax(s, axis=-1).astype(v.dtype)
        return jnp.einsum("bhqk,bkhd->bqhd", pr, v)

    out = lax.map(one_block, (jnp.arange(nb), qb))
    return out.transpose(1, 0, 2, 3, 4).reshape(bsz, seq, nh, v.shape[-1])


def mla_mixer(p, cos, sin, q_norm_g, kv_norm_g, w_uq, w_ukv, qk_g):
    bsz, seq, _ = p.shape
    c_q, c_kv, k_rope = _split(p, (MLA_Q_RANK, MLA_KV_RANK, MLA_ROPE))
    q = jnp.einsum("bsr,rc->bsc", rms_norm(c_q, q_norm_g), w_uq).reshape(bsz, seq, N_HEADS_GROUP, MLA_QK)
    kv = jnp.einsum("bsr,rc->bsc", rms_norm(c_kv, kv_norm_g), w_ukv).reshape(bsz, seq, N_HEADS_GROUP, MLA_NOPE + MLA_V)
    k_nope, v = kv[..., :MLA_NOPE], kv[..., MLA_NOPE:]
    k_pe = jnp.broadcast_to(k_rope[:, :, None, :], (bsz, seq, N_HEADS_GROUP, MLA_ROPE))
    k = jnp.concatenate([k_nope, k_pe], axis=-1)
    q = rms_norm(q, qk_g[0])
    k = rms_norm(k, qk_g[1])
    q = jnp.concatenate([q[..., :MLA_NOPE], apply_rope(q[..., MLA_NOPE:], cos, sin)], axis=-1)
    k = jnp.concatenate([k[..., :MLA_NOPE], apply_rope(k[..., MLA_NOPE:], cos, sin)], axis=-1)
    return causal_softmax_attention(q, k, v, MLA_QK ** -0.5)


def stick_breaking_mixer(p):
    bsz, seq, _ = p.shape
    q, k, v = [t.reshape(bsz, seq, N_HEADS_GROUP, HEAD_DIM) for t in _split(p, (GROUP_WIDTH,) * 3)]
    scale = HEAD_DIM ** -0.5
    nb = seq // SB_Q_BLOCK
    qb = q.reshape(bsz, nb, SB_Q_BLOCK, N_HEADS_GROUP, HEAD_DIM).transpose(1, 0, 2, 3, 4)
    kpos = jnp.arange(seq)

    def one_block(args):
        i, qi = args
        z = jnp.einsum("bqhd,bkhd->bhqk", qi, k).astype(jnp.float32) * scale
        qpos = i * SB_Q_BLOCK + jnp.arange(SB_Q_BLOCK)
        strict = kpos[None, :] < qpos[:, None]
        log_beta = jax.nn.log_sigmoid(z)
        log_1mb = jnp.where(strict, jax.nn.log_sigmoid(-z), 0.0)
        tail = lax.cumsum(log_1mb, axis=3, reverse=True) - log_1mb
        a = jnp.where(strict, jnp.exp(log_beta + tail), 0.0).astype(v.dtype)
        return jnp.einsum("bhqk,bkhd->bqhd", a, v)

    out = lax.map(one_block, (jnp.arange(nb), qb))
    return out.transpose(1, 0, 2, 3, 4).reshape(bsz, seq, N_HEADS_GROUP, HEAD_DIM)


def ssd_chunked(xh, dt, a_head, bm, cm):
    bsz, seq, nh, hp = xh.shape
    ns = bm.shape[-1]
    nc, cl = seq // SSM_CHUNK, SSM_CHUNK
    x = (xh * dt[..., None]).reshape(bsz, nc, cl, nh, hp)
    a = (dt * a_head).reshape(bsz, nc, cl, nh).transpose(0, 3, 1, 2)
    bc = bm.reshape(bsz, nc, cl, nh, ns)
    cc = cm.reshape(bsz, nc, cl, nh, ns)
    a_cum = jnp.cumsum(a, axis=-1)
    tri = jnp.tril(jnp.ones((cl, cl), dtype=bool))
    seg = a_cum[..., :, None] - a_cum[..., None, :]
    lmat = jnp.where(tri, jnp.exp(jnp.where(tri, seg, 0.0)), 0.0)
    y_diag = jnp.einsum("bclhn,bcshn,bhcls,bcshp->bclhp", cc, bc, lmat, x)
    decay_states = jnp.exp(a_cum[..., -1:] - a_cum)
    states = jnp.einsum("bclhn,bhcl,bclhp->bchpn", bc, decay_states, x)
    chunk_decay = jnp.exp(a_cum[..., -1])

    def step(h, inp):
        s_c, d_c = inp
        return d_c[..., None, None] * h + s_c, h

    h0 = jnp.zeros((bsz, nh, hp, ns), jnp.float32)
    _, prev = lax.scan(step, h0, (states.transpose(1, 0, 2, 3, 4), chunk_decay.transpose(2, 0, 1)))
    prev = prev.transpose(1, 0, 2, 3, 4)
    y_off = jnp.einsum("bclhn,bchpn,bhcl->bclhp", cc, prev, jnp.exp(a_cum))
    return (y_diag + y_off).reshape(bsz, seq, nh, hp)


def mamba2_mixer(p, conv_w, conv_b, dt_bias, a_log, d_skip, norm_g):
    bsz, seq, _ = p.shape
    z, xbc, dt_raw = _split(p, (GROUP_WIDTH, SSM_XBC, SSM_HEADS))
    xbc = jax.nn.silu(causal_dwconv(xbc, conv_w) + conv_b)
    xs, bm, cm = _split(xbc, (GROUP_WIDTH, SSM_GROUPS * SSM_STATE, SSM_GROUPS * SSM_STATE))
    rep = SSM_HEADS // SSM_GROUPS
    xh = xs.reshape(bsz, seq, SSM_HEADS, SSM_HEADDIM).astype(jnp.float32)
    bm = jnp.repeat(bm.reshape(bsz, seq, SSM_GROUPS, SSM_STATE), rep, axis=2).astype(jnp.float32)
    cm = jnp.repeat(cm.reshape(bsz, seq, SSM_GROUPS, SSM_STATE), rep, axis=2).astype(jnp.float32)
    dt = jax.nn.softplus(dt_raw.astype(jnp.float32) + dt_bias.astype(jnp.float32))
    a_head = -jnp.exp(a_log.astype(jnp.float32))
    y = ssd_chunked(xh, dt, a_head, bm, cm) + d_skip.astype(jnp.float32)[:, None] * xh
    y = y.reshape(bsz, seq, GROUP_WIDTH).astype(p.dtype)
    return rms_norm(y * jax.nn.silu(z), norm_g)


def setup_inputs(seed: int = 0) -> dict:
    key = jax.random.key(seed)
    ks = jax.random.split(key, 24)
    f32 = jnp.float32
    L = DEPTH

    def normal(k, shape, scale):
        return jax.random.normal(k, shape, f32) * scale

    def gain(k, shape):
        return 1.0 + 0.1 * jax.random.normal(k, shape, f32)

    x = normal(ks[0], (BATCH, SEQ, D_MODEL), 1.0)
    positions = (jax.random.randint(ks[1], (BATCH, 1), 0, 4096, dtype=jnp.int32)
                 + jnp.arange(SEQ, dtype=jnp.int32)[None, :])
    dt0 = jnp.exp(jax.random.uniform(ks[13], (L, SSM_HEADS), f32, math.log(1e-3), math.log(1e-1)))
    return {
        "x": x,
        "positions": positions,
        "mix_norm_g": gain(ks[2], (L, D_MODEL)),
        "w_in": normal(ks[3], (L, D_MODEL, IN_COLS), D_MODEL ** -0.5),
        "moba_qk_g": gain(ks[4], (L, 2, HEAD_DIM)),
        "mla_q_norm_g": gain(ks[5], (L, MLA_Q_RANK)),
        "mla_kv_norm_g": gain(ks[6], (L, MLA_KV_RANK)),
        "mla_w_uq": normal(ks[7], (L, MLA_Q_RANK, N_HEADS_GROUP * MLA_QK), MLA_Q_RANK ** -0.5),
        "mla_w_ukv": normal(ks[8], (L, MLA_KV_RANK, N_HEADS_GROUP * (MLA_NOPE + MLA_V)), MLA_KV_RANK ** -0.5),
        "mla_qk_g": gain(ks[9], (L, 2, MLA_QK)),
        "ssm_conv_w": normal(ks[10], (L, SSM_CONV, SSM_XBC), SSM_CONV ** -0.5),
        "ssm_conv_b": normal(ks[11], (L, SSM_XBC), 0.02),
        "ssm_dt_bias": dt0 + jnp.log(-jnp.expm1(-dt0)),
        "ssm_a_log": jnp.log(jax.random.uniform(ks[12], (L, SSM_HEADS), f32, 1.0, 16.0)),
        "ssm_d": gain(ks[14], (L, SSM_HEADS)),
        "ssm_norm_g": gain(ks[15], (L, GROUP_WIDTH)),
        "head_out_g": gain(ks[16], (L, 3, N_HEADS_GROUP, HEAD_DIM)),
        "w_out": normal(ks[17], (L, D_MIX, D_MODEL), D_MIX ** -0.5),
        "ffn_norm_g": gain(ks[18], (L, D_MODEL)),
        "ffn_w_in": normal(ks[19], (L, D_MODEL, 2 * D_FF), D_MODEL ** -0.5),
        "ffn_conv_w": normal(ks[20], (L, FFN_CONV, 2 * D_FF), FFN_CONV ** -0.5),
        "ffn_conv_b": normal(ks[21], (L, 2 * D_FF), 0.02),
        "ffn_w_out": normal(ks[22], (L, D_FF, D_MODEL), D_FF ** -0.5),
    }


def reference(x, positions, mix_norm_g, w_in, moba_qk_g, mla_q_norm_g, mla_kv_norm_g,
              mla_w_uq, mla_w_ukv, mla_qk_g, ssm_conv_w, ssm_conv_b, ssm_dt_bias, ssm_a_log,
              ssm_d, ssm_norm_g, head_out_g, w_out, ffn_norm_g, ffn_w_in, ffn_conv_w,
              ffn_conv_b, ffn_w_out):
    bsz, seq, _ = x.shape
    cos_a, sin_a = rope_tables(positions, HEAD_DIM)
    cos_m, sin_m = rope_tables(positions, MLA_ROPE)
    for l in range(DEPTH):
        h = rms_norm(x, mix_norm_g[l])
        proj = jnp.einsum("bsd,dc->bsc", h, w_in[l])
        moba_p, mla_p, sb_p, ssm_p = _split(proj, (MOBA_COLS, MLA_COLS, SB_COLS, SSM_COLS))
        o_moba = moba_mixer(moba_p, cos_a, sin_a, moba_qk_g[l])
        o_mla = mla_mixer(mla_p, cos_m, sin_m, mla_q_norm_g[l], mla_kv_norm_g[l],
                          mla_w_uq[l], mla_w_ukv[l], mla_qk_g[l])
        o_sb = stick_breaking_mixer(sb_p)
        o_ssm = mamba2_mixer(ssm_p, ssm_conv_w[l], ssm_conv_b[l], ssm_dt_bias[l],
                             ssm_a_log[l], ssm_d[l], ssm_norm_g[l])
        heads = rms_norm(jnp.stack([o_moba, o_mla, o_sb], axis=2), head_out_g[l])
        mixed = jnp.concatenate([heads.reshape(bsz, seq, 3 * GROUP_WIDTH), o_ssm], axis=-1)
        x = x + jnp.einsum("bsc,cd->bsd", mixed, w_out[l])
        h = rms_norm(x, ffn_norm_g[l])
        u = causal_dwconv(jnp.einsum("bsd,df->bsf", h, ffn_w_in[l]), ffn_conv_w[l]) + ffn_conv_b[l]
        gate, up = _split(u, (D_FF, D_FF))
        x = x + jnp.einsum("bsf,fd->bsd", jax.nn.silu(gate) * up, ffn_w_out[l])
    return x
```

```python
import functools
import math

import jax
import jax.numpy as jnp
from jax import lax
from jax.experimental import pallas as pl
from jax.experimental.pallas import tpu as pltpu

F32 = jnp.float32
BF16 = jnp.bfloat16

D_MODEL = 1024
SEQ = 2048
GROUP_WIDTH = 256
HEAD_DIM = 64
N_HEADS = 4
MOBA_BLOCK = 256
MOBA_TOPK = 3
MLA_Q_RANK = 192
MLA_KV_RANK = 128
MLA_NOPE = 64
MLA_ROPE = 32
MLA_QK = MLA_NOPE + MLA_ROPE
SSM_HEADS = 4
SSM_STATE = 64
SSM_CONV = 4
SSM_CHUNK = 128
SSM_XBC = 512
D_FF = 2816
FFN_CONV = 3
ROPE_THETA = 10000.0
EPS = 1e-6

LANES = 128
VMEM_LIMIT_BYTES = 56 * 1024 * 1024
MASK_VALUE = -1e30

C_MOBA = 0
C_CQ = 768
C_CKV = 1024
C_KR = 1152
C_SB = 1280
C_Z = 2048
C_XBC = 2304
IN_COLS_R = 2816
KR_LANE = 64
DT_LANE = 96

TM_PROJ = 512
TQ = 256
FFN_HALO = 16
FFN_CHUNKS = ((0, 512), (512, 512), (1024, 512), (1536, 512), (2048, 512), (2560, 256))


def _params(*sem):
    return pltpu.CompilerParams(dimension_semantics=sem, vmem_limit_bytes=VMEM_LIMIT_BYTES)


def _full(shape):
    n = len(shape)
    return pl.BlockSpec(shape, lambda *_: (0,) * n)


def _sigmoid(x):
    return 1.0 / (1.0 + jnp.exp(-x))


def _softplus_neg_abs(x):
    return jnp.log(1.0 + jnp.exp(-jnp.abs(x)))


def _group_sumsq(x, group):
    xx = x * x
    outs = []
    for c in range(x.shape[1] // LANES):
        slab = xx[:, c * LANES:(c + 1) * LANES]
        if group == LANES:
            s = jnp.sum(slab, axis=1, keepdims=True)
            outs.append(jnp.broadcast_to(s, slab.shape))
        else:
            lane = lax.broadcasted_iota(jnp.int32, slab.shape, 1)
            lo = lane < group
            s_lo = jnp.sum(jnp.where(lo, slab, 0.0), axis=1, keepdims=True)
            s_hi = jnp.sum(jnp.where(lo, 0.0, slab), axis=1, keepdims=True)
            outs.append(jnp.where(lo, s_lo, s_hi))
    return outs[0] if len(outs) == 1 else jnp.concatenate(outs, axis=1)


def _swap_halves(x, period, half, lo_start):
    width = x.shape[1]
    lane = lax.broadcasted_iota(jnp.int32, x.shape, 1) & (period - 1)
    first = (lane >= lo_start) & (lane < lo_start + half)
    return jnp.where(first, pltpu.roll(x, width - half, 1), pltpu.roll(x, half, 1))


def _nt_dot(a, b):
    return lax.dot_general(a, b, (((1,), (1,)), ((), ())), preferred_element_type=F32)


def _inproj_body(x_ref, g_ref, w_ref, cosa_ref, sina_ref, cosm_ref, sinm_ref, gmoba_ref, gcq_ref,
                 gckv_ref, wuq_ref, wuk_ref, wuv_ref, gmla_ref,
                 mq_ref, mk_ref, mv_ref, lq_ref, lk_ref, lv_ref, sq_ref, sk_ref, sv_ref,
                 z_ref, xbc_ref, dt_ref):
    x = x_ref[...]
    ms = jnp.mean(x * x, axis=-1, keepdims=True)
    h = (x * lax.rsqrt(ms + EPS) * g_ref[...]).astype(BF16)

    def proj(lo, width):
        return jnp.dot(h, w_ref[:, lo:lo + width], preferred_element_type=F32)

    cosa = jnp.concatenate([cosa_ref[...]] * 2, axis=1)
    sina = jnp.concatenate([sina_ref[...]] * 2, axis=1)
    for idx, dst in ((0, mq_ref), (1, mk_ref)):
        t = proj(C_MOBA + idx * GROUP_WIDTH, GROUP_WIDTH)
        t = t * lax.rsqrt(_group_sumsq(t, HEAD_DIM) * (1.0 / HEAD_DIM) + EPS) * gmoba_ref[idx:idx + 1, :]
        t = t * cosa + _swap_halves(t, HEAD_DIM, HEAD_DIM // 2, 0) * sina
        dst[...] = t.astype(BF16)
    mv_ref[...] = proj(C_MOBA + 2 * GROUP_WIDTH, GROUP_WIDTH).astype(BF16)

    cq = proj(C_CQ, 256)
    cq = cq * lax.rsqrt(jnp.sum(cq * cq, axis=-1, keepdims=True) * (1.0 / MLA_Q_RANK) + EPS) * gcq_ref[...]
    ckv = proj(C_CKV, LANES)
    ckv = (ckv * lax.rsqrt(jnp.mean(ckv * ckv, axis=-1, keepdims=True) + EPS) * gckv_ref[...]).astype(BF16)
    krdt = proj(C_KR, LANES)
    dt_ref[...] = krdt
    lane = lax.broadcasted_iota(jnp.int32, krdt.shape, 1)
    kr = jnp.where((lane >= KR_LANE) & (lane < KR_LANE + MLA_ROPE), krdt, 0.0)
    cosm = jnp.concatenate([cosm_ref[...]] * N_HEADS, axis=1)
    sinm = jnp.concatenate([sinm_ref[...]] * N_HEADS, axis=1)
    ql = jnp.dot(cq.astype(BF16), wuq_ref[...], preferred_element_type=F32)
    kl = jnp.dot(ckv, wuk_ref[...], preferred_element_type=F32) + jnp.concatenate([kr] * N_HEADS, axis=1)
    for idx, t, dst in ((0, ql, lq_ref), (1, kl, lk_ref)):
        t = t * lax.rsqrt(_group_sumsq(t, LANES) * (1.0 / MLA_QK) + EPS) * gmla_ref[idx:idx + 1, :]
        t = t * cosm + _swap_halves(t, LANES, MLA_ROPE // 2, MLA_NOPE) * sinm
        dst[...] = t.astype(BF16)
    lv_ref[...] = jnp.dot(ckv, wuv_ref[...], preferred_element_type=F32).astype(BF16)

    sq_ref[...] = proj(C_SB, GROUP_WIDTH).astype(BF16)
    sk_ref[...] = proj(C_SB + GROUP_WIDTH, GROUP_WIDTH).astype(BF16)
    sv_ref[...] = proj(C_SB + 2 * GROUP_WIDTH, GROUP_WIDTH).astype(BF16)

    z_ref[...] = proj(C_Z, GROUP_WIDTH)
    xbc_ref[...] = proj(C_XBC, SSM_XBC)


def _inproj(x2d, prm):
    t = x2d.shape[0]
    tm = TM_PROJ
    row = lambda w: pl.BlockSpec((tm, w), lambda i: (i, 0))
    out_shapes = [
        jax.ShapeDtypeStruct((t, 256), BF16), jax.ShapeDtypeStruct((t, 256), BF16), jax.ShapeDtypeStruct((t, 256), BF16),
        jax.ShapeDtypeStruct((t, 512), BF16), jax.ShapeDtypeStruct((t, 512), BF16), jax.ShapeDtypeStruct((t, 256), BF16),
        jax.ShapeDtypeStruct((t, 256), BF16), jax.ShapeDtypeStruct((t, 256), BF16), jax.ShapeDtypeStruct((t, 256), BF16),
        jax.ShapeDtypeStruct((t, 256), F32), jax.ShapeDtypeStruct((t, 512), F32), jax.ShapeDtypeStruct((t, 128), F32),
    ]
    out_specs = [row(s.shape[1]) for s in out_shapes]
    consts = [prm["mix_g"], prm["w_in"]]
    tables = [prm["cosa"], prm["sina"], prm["cosm"], prm["sinm"]]
    tail = [prm["gmoba"], prm["gcq"], prm["gckv"], prm["wuq"], prm["wuk"], prm["wuv"], prm["gmla"]]
    in_specs = ([row(D_MODEL)] + [_full(a.shape) for a in consts] + [row(LANES)] * 4
                + [_full(a.shape) for a in tail])
    return pl.pallas_call(
        _inproj_body, grid=(t // tm,), in_specs=in_specs, out_specs=out_specs, out_shape=out_shapes,
        compiler_params=_params("arbitrary"), name="inproj",
    )(x2d, *consts, *tables, *tail)


def _head_out_norm(o, gain):
    ms = _group_sumsq(o, HEAD_DIM) * (1.0 / HEAD_DIM)
    return (o * lax.rsqrt(ms + EPS) * gain).astype(BF16)


def _tile_iotas():
    row = lax.broadcasted_iota(jnp.int32, (TQ, TQ), 0)
    col = lax.broadcasted_iota(jnp.int32, (TQ, TQ), 1)
    return row, col


def _moba_body(q_ref, k_ref, v_ref, gout_ref, o_ref, kmean_ref):
    i = pl.program_id(1)
    nblk = SEQ // MOBA_BLOCK

    @pl.when(i == 0)
    def _():
        kmean_ref[...] = jnp.zeros(kmean_ref.shape, F32)
        for n in range(nblk):
            kb = k_ref[0, n * MOBA_BLOCK:(n + 1) * MOBA_BLOCK, :].astype(F32)
            kmean_ref[n:n + 1, :] = jnp.mean(kb, axis=0, keepdims=True)

    q = q_ref[0]
    row, col = _tile_iotas()
    head_of_lane = col >> 6
    lane_g = lax.broadcasted_iota(jnp.int32, (TQ, LANES), 1)
    kd = k_ref[0, pl.ds(pl.multiple_of(i * TQ, TQ), TQ), :]
    vd = v_ref[0, pl.ds(pl.multiple_of(i * TQ, TQ), TQ), :]
    out = jnp.zeros((TQ, GROUP_WIDTH), F32)
    for hd in range(N_HEADS):
        hm = head_of_lane == hd
        qh = jnp.where(hm, q, jnp.zeros_like(q))
        km = kmean_ref[...]
        km_hi = km.astype(BF16)
        km_lo = (km - km_hi.astype(F32)).astype(BF16)
        gate = _nt_dot(qh, km_hi) + _nt_dot(qh, km_lo)
        rank = jnp.zeros((TQ, LANES), F32)
        for m in range(nblk - 1):
            gm = gate[:, m:m + 1]
            beats = (gm > gate) | ((gm == gate) & (lane_g > m))
            rank = rank + jnp.where(beats & (i > m), 1.0, 0.0)
        sel = jnp.where((lane_g < i) & (rank < float(MOBA_TOPK)), 1.0, 0.0)

        s = _nt_dot(qh, kd)
        s = jnp.where(col <= row, s, MASK_VALUE)
        m_run = jnp.max(s, axis=1, keepdims=True)
        p = jnp.exp(s - m_run)
        l_run = jnp.sum(p, axis=1, keepdims=True)
        acc = jnp.dot(p.astype(BF16), vd, preferred_element_type=F32)

        def body(j, carry):
            m_run, l_run, acc = carry
            kj = k_ref[0, pl.ds(pl.multiple_of(j * TQ, TQ), TQ), :]
            vj = v_ref[0, pl.ds(pl.multiple_of(j * TQ, TQ), TQ), :]
            selcol = jnp.sum(jnp.where(lane_g == j, sel, 0.0), axis=1, keepdims=True)
            s = jnp.where(selcol > 0.5, _nt_dot(qh, kj), MASK_VALUE)
            m_new = jnp.maximum(m_run, jnp.max(s, axis=1, keepdims=True))
            alpha = jnp.exp(m_run - m_new)
            p = jnp.exp(s - m_new)
            l_new = alpha * l_run + jnp.sum(p, axis=1, keepdims=True)
            acc = alpha * acc + jnp.dot(p.astype(BF16), vj, preferred_element_type=F32)
            return m_new, l_new, acc

        m_run, l_run, acc = lax.fori_loop(0, i, body, (m_run, l_run, acc))
        out = jnp.where(hm, acc * (1.0 / l_run), out)
    o_ref[0] = _head_out_norm(out, gout_ref[...])


def _moba(q, k, v, gout):
    b = q.shape[0]
    return pl.pallas_call(
        _moba_body, grid=(b, SEQ // TQ),
        in_specs=[pl.BlockSpec((1, TQ, 256), lambda bi, i: (bi, i, 0)),
                  pl.BlockSpec((1, SEQ, 256), lambda bi, i: (bi, 0, 0)),
                  pl.BlockSpec((1, SEQ, 256), lambda bi, i: (bi, 0, 0)),
                  _full(gout.shape)],
        out_specs=pl.BlockSpec((1, TQ, 256), lambda bi, i: (bi, i, 0)),
        out_shape=jax.ShapeDtypeStruct((b, SEQ, 256), BF16),
        scratch_shapes=[pltpu.VMEM((LANES, 256), F32)],
        compiler_params=_params("arbitrary", "arbitrary"), name="moba",
    )(q, k, v, gout)


def _mla_body(q_ref, k_ref, v_ref, gout_ref, o_ref):
    i = pl.program_id(1)
    row, col = _tile_iotas()
    head_of_lane = col >> 6
    d0 = pl.multiple_of(i * TQ, TQ)
    vd = v_ref[0, pl.ds(d0, TQ), :]
    out = jnp.zeros((TQ, GROUP_WIDTH), F32)
    for hd in range(N_HEADS):
        hl = slice(hd * LANES, (hd + 1) * LANES)
        qh = q_ref[0, :, hl]
        s = _nt_dot(qh, k_ref[0, pl.ds(d0, TQ), hl])
        s = jnp.where(col <= row, s, MASK_VALUE)
        m_run = jnp.max(s, axis=1, keepdims=True)
        p = jnp.exp(s - m_run)
        l_run = jnp.sum(p, axis=1, keepdims=True)
        acc = jnp.dot(p.astype(BF16), vd, preferred_element_type=F32)

        def body(j, carry):
            m_run, l_run, acc = carry
            j0 = pl.multiple_of(j * TQ, TQ)
            s = _nt_dot(qh, k_ref[0, pl.ds(j0, TQ), hl])
            m_new = jnp.maximum(m_run, jnp.max(s, axis=1, keepdims=True))
            alpha = jnp.exp(m_run - m_new)
            p = jnp.exp(s - m_new)
            l_new = alpha * l_run + jnp.sum(p, axis=1, keepdims=True)
            acc = alpha * acc + jnp.dot(p.astype(BF16), v_ref[0, pl.ds(j0, TQ), :], preferred_element_type=F32)
            return m_new, l_new, acc

        m_run, l_run, acc = lax.fori_loop(0, i, body, (m_run, l_run, acc))
        out = jnp.where(head_of_lane == hd, acc * (1.0 / l_run), out)
    o_ref[0] = _head_out_norm(out, gout_ref[...])


def _mla(q, k, v, gout):
    b = q.shape[0]
    return pl.pallas_call(
        _mla_body, grid=(b, SEQ // TQ),
        in_specs=[pl.BlockSpec((1, TQ, 512), lambda bi, i: (bi, i, 0)),
                  pl.BlockSpec((1, SEQ, 512), lambda bi, i: (bi, 0, 0)),
                  pl.BlockSpec((1, SEQ, 256), lambda bi, i: (bi, 0, 0)),
                  _full(gout.shape)],
        out_specs=pl.BlockSpec((1, TQ, 256), lambda bi, i: (bi, i, 0)),
        out_shape=jax.ShapeDtypeStruct((b, SEQ, 256), BF16),
        compiler_params=_params("arbitrary", "arbitrary"), name="mla",
    )(q, k, v, gout)


def _sb_body(q_ref, k_ref, v_ref, uu_ref, gout_ref, o_ref):
    i = pl.program_id(1)
    row, col = _tile_iotas()
    head_of_lane = col >> 6
    strict = col < row
    q = q_ref[0]
    d0 = pl.multiple_of(i * TQ, TQ)
    kd = k_ref[0, pl.ds(d0, TQ), :]
    vd = v_ref[0, pl.ds(d0, TQ), :]
    uu = uu_ref[...]

    def logs(z):
        sp = _softplus_neg_abs(z)
        return jnp.minimum(z, 0.0) - sp, -jnp.maximum(z, 0.0) - sp

    def suffix_sum(l1):
        hi = l1.astype(BF16)
        lo = (l1 - hi.astype(F32)).astype(BF16)
        return jnp.dot(jnp.concatenate([hi, lo], axis=1), uu, preferred_element_type=F32)

    out = jnp.zeros((TQ, GROUP_WIDTH), F32)
    for hd in range(N_HEADS):
        hm = head_of_lane == hd
        qh = jnp.where(hm, q, jnp.zeros_like(q))
        lb, l1 = logs(_nt_dot(qh, kd))
        l1 = jnp.where(strict, l1, 0.0)
        a = jnp.where(strict, jnp.exp(lb + suffix_sum(l1)), 0.0)
        acc = jnp.dot(a.astype(BF16), vd, preferred_element_type=F32)
        csum = jnp.sum(l1, axis=1, keepdims=True)

        def body(jj, carry):
            csum, acc = carry
            j0 = pl.multiple_of((i - 1 - jj) * TQ, TQ)
            lb, l1 = logs(_nt_dot(qh, k_ref[0, pl.ds(j0, TQ), :]))
            a = jnp.exp(lb + (suffix_sum(l1) + csum))
            acc = acc + jnp.dot(a.astype(BF16), v_ref[0, pl.ds(j0, TQ), :], preferred_element_type=F32)
            return csum + jnp.sum(l1, axis=1, keepdims=True), acc

        csum, acc = lax.fori_loop(0, i, body, (csum, acc))
        out = jnp.where(hm, acc, out)
    o_ref[0] = _head_out_norm(out, gout_ref[...])


def _sb(q, k, v, gout):
    b = q.shape[0]
    r = lax.broadcasted_iota(jnp.int32, (2 * TQ, TQ), 0) & (TQ - 1)
    c = lax.broadcasted_iota(jnp.int32, (2 * TQ, TQ), 1)
    uu = jnp.where(r > c, 1.0, 0.0).astype(BF16)
    return pl.pallas_call(
        _sb_body, grid=(b, SEQ // TQ),
        in_specs=[pl.BlockSpec((1, TQ, 256), lambda bi, i: (bi, i, 0)),
                  pl.BlockSpec((1, SEQ, 256), lambda bi, i: (bi, 0, 0)),
                  pl.BlockSpec((1, SEQ, 256), lambda bi, i: (bi, 0, 0)),
                  _full(uu.shape), _full(gout.shape)],
        out_specs=pl.BlockSpec((1, TQ, 256), lambda bi, i: (bi, i, 0)),
        out_shape=jax.ShapeDtypeStruct((b, SEQ, 256), BF16),
        compiler_params=_params("arbitrary", "arbitrary"), name="stickbreak",
    )(q, k, v, uu, gout)


def _ssd_body(z_ref, xbc_ref, dt_ref, cw_ref, cb_ref, dtb_ref, alog_ref, dskip_ref, ng_ref, o_ref,
              halo_ref, st_ref):
    c = pl.program_id(1)
    cl = SSM_CHUNK

    @pl.when(c == 0)
    def _():
        halo_ref[...] = jnp.zeros(halo_ref.shape, F32)
        st_ref[...] = jnp.zeros(st_ref.shape, F32)

    xb = xbc_ref[0]
    ext = jnp.concatenate([halo_ref[...], xb], axis=0)
    u = cb_ref[...] + cw_ref[SSM_CONV - 1:SSM_CONV, :] * xb
    for tap in range(SSM_CONV - 1):
        shift = SSM_CONV - 1 - tap
        u = u + cw_ref[tap:tap + 1, :] * pltpu.roll(ext, shift, 0)[8:8 + cl]
    halo_ref[...] = xb[cl - 8:cl]
    xc = u * _sigmoid(u)
    xs = xc[:, 0:GROUP_WIDTH]
    bm = xc[:, GROUP_WIDTH:GROUP_WIDTH + LANES]
    cm = xc[:, GROUP_WIDTH + LANES:SSM_XBC]

    lane = lax.broadcasted_iota(jnp.int32, (cl, LANES), 1)
    head_lane = (lane >= DT_LANE) & (lane < DT_LANE + SSM_HEADS)
    dtv = dt_ref[0] + dtb_ref[...]
    dt = jnp.where(head_lane, jnp.maximum(dtv, 0.0) + _softplus_neg_abs(dtv), 0.0)
    a = dt * (-jnp.exp(alog_ref[...]))

    r = lax.broadcasted_iota(jnp.int32, (cl, cl), 0)
    s = lax.broadcasted_iota(jnp.int32, (cl, cl), 1)
    tril = s <= r
    tril_b = jnp.where(tril, 1.0, 0.0).astype(BF16)
    a1 = a.astype(BF16)
    a2 = (a - a1.astype(F32)).astype(BF16)
    a3 = (a - a1.astype(F32) - a2.astype(F32)).astype(BF16)
    acum = (jnp.dot(tril_b, a1, preferred_element_type=F32)
            + (jnp.dot(tril_b, a2, preferred_element_type=F32)
               + jnp.dot(tril_b, a3, preferred_element_type=F32)))
    acum_t = acum.T

    lane2 = lax.broadcasted_iota(jnp.int32, (cl, GROUP_WIDTH), 1)
    head2 = lane2 >> 6

    def per_head(v):
        out = jnp.broadcast_to(v[:, DT_LANE + 3:DT_LANE + 4], (cl, GROUP_WIDTH))
        for hd in (2, 1, 0):
            out = jnp.where(head2 == hd, v[:, DT_LANE + hd:DT_LANE + hd + 1], out)
        return out

    dt_full = per_head(dt)
    acum_full = per_head(acum)
    last_full = acum_full[cl - 1:cl, :]
    xdt = xs * dt_full
    xdt_b = xdt.astype(BF16)

    grp = lane >> 6
    bm_b = bm.astype(BF16)
    y = jnp.zeros((cl, GROUP_WIDTH), F32)
    for g in range(2):
        cg = jnp.where(grp == g, cm, 0.0).astype(BF16)
        gmat = _nt_dot(cg, bm_b)
        for hd in (2 * g, 2 * g + 1):
            seg = acum[:, DT_LANE + hd:DT_LANE + hd + 1] - acum_t[DT_LANE + hd:DT_LANE + hd + 1, :]
            lmat = jnp.where(tril, jnp.exp(jnp.where(tril, seg, 0.0)), 0.0)
            yd = jnp.dot((gmat * lmat).astype(BF16), xdt_b, preferred_element_type=F32)
            y = jnp.where(head2 == hd, yd, y)

    st_prev = st_ref[...]
    y_off = jnp.exp(acum_full) * jnp.dot(cm.astype(BF16), st_prev.astype(BF16), preferred_element_type=F32)
    dec = jnp.exp(last_full - acum_full)
    s_new = jnp.dot(bm.T.astype(BF16), (xdt * dec).astype(BF16), preferred_element_type=F32)
    rs = lax.broadcasted_iota(jnp.int32, (LANES, GROUP_WIDTH), 0) >> 6
    cs = lax.broadcasted_iota(jnp.int32, (LANES, GROUP_WIDTH), 1) >> 7
    st_ref[...] = jnp.exp(last_full) * st_prev + jnp.where(rs == cs, s_new, 0.0)

    y = y + y_off + dskip_ref[...] * xs
    zz = z_ref[0]
    y = y * (zz * _sigmoid(zz))
    ms = jnp.mean(y * y, axis=-1, keepdims=True)
    o_ref[0] = (y * lax.rsqrt(ms + EPS) * ng_ref[...]).astype(BF16)


def _ssd(z, xbc, dt, prm):
    b = z.shape[0]
    cl = SSM_CHUNK
    small = [prm["ssm_cw"], prm["ssm_cb"], prm["ssm_dtb"], prm["ssm_alog"], prm["ssm_d"], prm["ssm_ng"]]
    return pl.pallas_call(
        _ssd_body, grid=(b, SEQ // cl),
        in_specs=[pl.BlockSpec((1, cl, 256), lambda bi, c: (bi, c, 0)),
                  pl.BlockSpec((1, cl, 512), lambda bi, c: (bi, c, 0)),
                  pl.BlockSpec((1, cl, LANES), lambda bi, c: (bi, c, 0))]
                 + [_full(a.shape) for a in small],
        out_specs=pl.BlockSpec((1, cl, 256), lambda bi, c: (bi, c, 0)),
        out_shape=jax.ShapeDtypeStruct((b, SEQ, 256), BF16),
        scratch_shapes=[pltpu.VMEM((8, SSM_XBC), F32), pltpu.VMEM((LANES, GROUP_WIDTH), F32)],
        compiler_params=_params("arbitrary", "arbitrary"), name="ssd",
    )(z, xbc, dt, *small)


def _outproj_body(x_ref, o1_ref, o2_ref, o3_ref, o4_ref, w_ref, y_ref):
    acc = x_ref[...]
    for n, o_ref in enumerate((o1_ref, o2_ref, o3_ref, o4_ref)):
        acc = acc + jnp.dot(o_ref[...], w_ref[n * GROUP_WIDTH:(n + 1) * GROUP_WIDTH, :],
                            preferred_element_type=F32)
    y_ref[...] = acc


def _outproj(x2d, outs, w_out):
    t = x2d.shape[0]
    tm = TM_PROJ
    return pl.pallas_call(
        _outproj_body, grid=(t // tm,),
        in_specs=[pl.BlockSpec((tm, D_MODEL), lambda i: (i, 0))]
                 + [pl.BlockSpec((tm, GROUP_WIDTH), lambda i: (i, 0))] * 4 + [_full(w_out.shape)],
        out_specs=pl.BlockSpec((tm, D_MODEL), lambda i: (i, 0)),
        out_shape=jax.ShapeDtypeStruct((t, D_MODEL), F32),
        compiler_params=_params("arbitrary"), name="outproj",
    )(x2d, *outs, w_out)


def _ffn_body(x_ref, halo_ref, g_ref, win_ref, cw_ref, cb_ref, wout_ref, y_ref, act_ref):
    tm = TM_PROJ
    first = (pl.program_id(0) % (SEQ // tm)) == 0
    g = g_ref[...]

    def norm(v):
        ms = jnp.mean(v * v, axis=-1, keepdims=True)
        return v * lax.rsqrt(ms + EPS) * g

    x = x_ref[...]
    h_halo = jnp.where(first, 0.0, norm(halo_ref[...])).astype(BF16)
    h_ext = jnp.concatenate([h_halo, norm(x).astype(BF16)], axis=0)

    def conv(u, lo, width):
        w = cw_ref[:, lo:lo + width]
        out = cb_ref[:, lo:lo + width] + w[2:3, :] * u[FFN_HALO:, :]
        out = out + w[1:2, :] * pltpu.roll(u, 1, 0)[FFN_HALO:, :]
        return out + w[0:1, :] * pltpu.roll(u, 2, 0)[FFN_HALO:, :]

    for lo, width in FFN_CHUNKS:
        gate = conv(jnp.dot(h_ext, win_ref[:, lo:lo + width], preferred_element_type=F32), lo, width)
        up = conv(jnp.dot(h_ext, win_ref[:, D_FF + lo:D_FF + lo + width], preferred_element_type=F32),
                  D_FF + lo, width)
        act_ref[:, lo:lo + width] = (gate * _sigmoid(gate) * up).astype(BF16)
    y_ref[...] = x + jnp.dot(act_ref[...], wout_ref[...], preferred_element_type=F32)


def _ffn(x2d, prm):
    t = x2d.shape[0]
    tm = TM_PROJ
    per = tm // FFN_HALO
    consts = [prm["ffn_g"], prm["ffn_win"], prm["ffn_cw"], prm["ffn_cb"], prm["ffn_wout"]]
    return pl.pallas_call(
        _ffn_body, grid=(t // tm,),
        in_specs=[pl.BlockSpec((tm, D_MODEL), lambda i: (i, 0)),
                  pl.BlockSpec((FFN_HALO, D_MODEL), lambda i: (jnp.maximum(i * per - 1, 0), 0))]
                 + [_full(a.shape) for a in consts],
        out_specs=pl.BlockSpec((tm, D_MODEL), lambda i: (i, 0)),
        out_shape=jax.ShapeDtypeStruct((t, D_MODEL), F32),
        scratch_shapes=[pltpu.VMEM((tm, D_FF), BF16)],
        compiler_params=_params("arbitrary"), name="ffn",
    )(x2d, x2d, *consts)


def _rope_tables(positions):
    pos = positions.reshape(-1).astype(F32)[:, None]
    inv_a = 1.0 / (ROPE_THETA ** (jnp.arange(0, HEAD_DIM, 2, dtype=F32) / HEAD_DIM))
    ang_a = pos * inv_a
    cosa = jnp.tile(jnp.cos(ang_a), (1, 4))
    sina = jnp.tile(jnp.concatenate([-jnp.sin(ang_a), jnp.sin(ang_a)], axis=1), (1, 2))
    inv_m = 1.0 / (ROPE_THETA ** (jnp.arange(0, MLA_ROPE, 2, dtype=F32) / MLA_ROPE))
    ang_m = pos * inv_m
    n = pos.shape[0]
    ones = jnp.ones((n, MLA_NOPE), F32)
    zeros32 = jnp.zeros((n, LANES - MLA_QK), F32)
    cosm = jnp.concatenate([ones, jnp.cos(ang_m), jnp.cos(ang_m), 1.0 + zeros32], axis=1)
    sinm = jnp.concatenate([0.0 * ones, -jnp.sin(ang_m), jnp.sin(ang_m), zeros32], axis=1)
    return cosa, sina, cosm, sinm


def _pad_heads(w, per_head, width):
    lead = w.shape[:-1]
    w = w.reshape(*lead, N_HEADS, per_head)
    w = jnp.pad(w, [(0, 0)] * len(lead) + [(0, 0), (0, width - per_head)])
    return w.reshape(*lead, N_HEADS * width)


def _layer_params(l, mix_norm_g, w_in, moba_qk_g, mla_q_norm_g, mla_kv_norm_g, mla_w_uq, mla_w_ukv,
                  mla_qk_g, ssm_conv_w, ssm_conv_b, ssm_dt_bias, ssm_a_log, ssm_d, ssm_norm_g,
                  head_out_g, w_out, ffn_norm_g, ffn_w_in, ffn_conv_w, ffn_conv_b, ffn_w_out):
    w = w_in[l]
    o = 0
    moba_w = w[:, o:o + 768]; o += 768
    cq_w = w[:, o:o + MLA_Q_RANK]; o += MLA_Q_RANK
    ckv_w = w[:, o:o + MLA_KV_RANK]; o += MLA_KV_RANK
    kr_w = w[:, o:o + MLA_ROPE]; o += MLA_ROPE
    sb_w = w[:, o:o + 768]; o += 768
    z_w = w[:, o:o + 256]; o += 256
    xbc_w = w[:, o:o + SSM_XBC]; o += SSM_XBC
    dt_w = w[:, o:o + SSM_HEADS]
    zc = lambda n: jnp.zeros((D_MODEL, n), F32)
    sb_scale = jnp.concatenate([jnp.full((256,), HEAD_DIM ** -0.5, F32), jnp.ones((512,), F32)])
    w_r = jnp.concatenate([
        moba_w, cq_w, zc(256 - MLA_Q_RANK), ckv_w,
        zc(KR_LANE), kr_w, dt_w, zc(LANES - DT_LANE - SSM_HEADS),
        sb_w * sb_scale, z_w, xbc_w], axis=1).astype(BF16)

    lane_row = lambda v, start: jnp.zeros((1, LANES), F32).at[0, start:start + v.shape[0]].set(v)
    moba_scale = jnp.array([[HEAD_DIM ** -0.5], [1.0]], F32)
    mla_scale = jnp.array([[MLA_QK ** -0.5], [1.0]], F32)
    ukv = mla_w_ukv[l].reshape(MLA_KV_RANK, N_HEADS, MLA_NOPE + HEAD_DIM)
    hog = head_out_g[l].reshape(3, 1, GROUP_WIDTH)
    return {
        "mix_g": mix_norm_g[l][None, :], "w_in": w_r,
        "gmoba": jnp.tile(moba_qk_g[l], (1, N_HEADS)) * moba_scale,
        "gcq": jnp.pad(mla_q_norm_g[l], (0, 256 - MLA_Q_RANK))[None, :],
        "gckv": mla_kv_norm_g[l][None, :],
        "wuq": jnp.pad(_pad_heads(mla_w_uq[l], MLA_QK, LANES), ((0, 256 - MLA_Q_RANK), (0, 0))).astype(BF16),
        "wuk": _pad_heads(ukv[:, :, :MLA_NOPE].reshape(MLA_KV_RANK, -1), MLA_NOPE, LANES).astype(BF16),
        "wuv": ukv[:, :, MLA_NOPE:].reshape(MLA_KV_RANK, GROUP_WIDTH).astype(BF16),
        "gmla": jnp.tile(jnp.pad(mla_qk_g[l], ((0, 0), (0, LANES - MLA_QK))), (1, N_HEADS)) * mla_scale,
        "ssm_cw": ssm_conv_w[l], "ssm_cb": ssm_conv_b[l][None, :],
        "ssm_dtb": lane_row(ssm_dt_bias[l], DT_LANE), "ssm_alog": lane_row(ssm_a_log[l], DT_LANE),
        "ssm_d": jnp.repeat(ssm_d[l], HEAD_DIM)[None, :], "ssm_ng": ssm_norm_g[l][None, :],
        "g_moba_out": hog[0], "g_mla_out": hog[1], "g_sb_out": hog[2],
        "w_out": w_out[l].astype(BF16),
        "ffn_g": ffn_norm_g[l][None, :], "ffn_win": ffn_w_in[l].astype(BF16),
        "ffn_cw": ffn_conv_w[l], "ffn_cb": ffn_conv_b[l][None, :], "ffn_wout": ffn_w_out[l].astype(BF16),
    }


def _layer(x2d, bsz, prm):
    (mq, mk, mv, lq, lk, lv, sq, sk, sv, z, xbc, dt) = _inproj(x2d, prm)
    r3 = lambda a: a.reshape(bsz, SEQ, a.shape[-1])
    o_moba = _moba(r3(mq), r3(mk), r3(mv), prm["g_moba_out"])
    o_mla = _mla(r3(lq), r3(lk), r3(lv), prm["g_mla_out"])
    o_sb = _sb(r3(sq), r3(sk), r3(sv), prm["g_sb_out"])
    o_ssm = _ssd(r3(z), r3(xbc), r3(dt), prm)
    flat = lambda a: a.reshape(bsz * SEQ, GROUP_WIDTH)
    x2d = _outproj(x2d, [flat(o_moba), flat(o_mla), flat(o_sb), flat(o_ssm)], prm["w_out"])
    return _ffn(x2d, prm)


def kernel(x, positions, mix_norm_g, w_in, moba_qk_g, mla_q_norm_g, mla_kv_norm_g, mla_w_uq, mla_w_ukv,
           mla_qk_g, ssm_conv_w, ssm_conv_b, ssm_dt_bias, ssm_a_log, ssm_d, ssm_norm_g, head_out_g, w_out,
           ffn_norm_g, ffn_w_in, ffn_conv_w, ffn_conv_b, ffn_w_out):
    bsz, seq, d = x.shape
    assert seq == SEQ and d == D_MODEL
    cosa, sina, cosm, sinm = _rope_tables(positions)
    x2d = x.reshape(bsz * seq, d)
    for l in range(w_in.shape[0]):
        prm = _layer_params(l, mix_norm_g, w_in, moba_qk_g, mla_q_norm_g, mla_kv_norm_g, mla_w_uq,
                            mla_w_ukv, mla_qk_g, ssm_conv_w, ssm_conv_b, ssm_dt_bias, ssm_a_log, ssm_d,
                            ssm_norm_g, head_out_g, w_out, ffn_norm_g, ffn_w_in, ffn_conv_w, ffn_conv_b,
                            ffn_w_out)
        prm.update(cosa=cosa, sina=sina, cosm=cosm, sinm=sinm)
        x2d = _layer(x2d, bsz, prm)
    return x2d.reshape(bsz, seq, d)
```

```python
import functools
import math

import jax
import jax.numpy as jnp
from jax import lax
from jax.experimental import pallas as pl
from jax.experimental.pallas import tpu as pltpu

F32 = jnp.float32
BF16 = jnp.bfloat16

D_MODEL = 1024
SEQ = 2048
GROUP_WIDTH = 256
HEAD_DIM = 64
N_HEADS = 4
MOBA_BLOCK = 256
MOBA_TOPK = 3
MLA_Q_RANK = 192
MLA_KV_RANK = 128
MLA_NOPE = 64
MLA_ROPE = 32
MLA_QK = MLA_NOPE + MLA_ROPE
SSM_HEADS = 4
SSM_STATE = 64
SSM_CONV = 4
SSM_CHUNK = 128
SSM_XBC = 512
D_FF = 2816
FFN_CONV = 3
ROPE_THETA = 10000.0
EPS = 1e-6

LANES = 128
VMEM_LIMIT_BYTES = 56 * 1024 * 1024
MASK_VALUE = -1e30

C_MOBA = 0
C_CQ = 512
C_CKV = 768
C_KR = 896
C_SB = 1024
C_Z = 1536
C_XBC = 1792
KR_LANE = 64
DT_LANE = 96

TM_PROJ = 512
TQ = 256
FFN_HALO = 16
FFN_CHUNKS = ((0, 512), (512, 512), (1024, 512), (1536, 512), (2048, 512), (2560, 256))


def _params(*sem):
    return pltpu.CompilerParams(dimension_semantics=sem, vmem_limit_bytes=VMEM_LIMIT_BYTES)


def _full(shape):
    n = len(shape)
    return pl.BlockSpec(shape, lambda *_: (0,) * n)


def _sigmoid(x):
    return 1.0 / (1.0 + jnp.exp(-x))


def _softplus_neg_abs(x):
    return jnp.log(1.0 + jnp.exp(-jnp.abs(x)))


def _group_sumsq(x, group):
    xx = x * x
    outs = []
    for c in range(x.shape[1] // LANES):
        slab = xx[:, c * LANES:(c + 1) * LANES]
        if group == LANES:
            s = jnp.sum(slab, axis=1, keepdims=True)
            outs.append(jnp.broadcast_to(s, slab.shape))
        else:
            lane = lax.broadcasted_iota(jnp.int32, slab.shape, 1)
            lo = lane < group
            s_lo = jnp.sum(jnp.where(lo, slab, 0.0), axis=1, keepdims=True)
            s_hi = jnp.sum(jnp.where(lo, 0.0, slab), axis=1, keepdims=True)
            outs.append(jnp.where(lo, s_lo, s_hi))
    return outs[0] if len(outs) == 1 else jnp.concatenate(outs, axis=1)


def _swap_halves(x, period, half, lo_start):
    width = x.shape[1]
    lane = lax.broadcasted_iota(jnp.int32, x.shape, 1) & (period - 1)
    first = (lane >= lo_start) & (lane < lo_start + half)
    return jnp.where(first, pltpu.roll(x, width - half, 1), pltpu.roll(x, half, 1))


def _nt_dot(a, b):
    return lax.dot_general(a, b, (((1,), (1,)), ((), ())), preferred_element_type=F32)


def _inproj_body(x_ref, g_ref, w_ref, wvt_ref, cosa_ref, sina_ref, cosm_ref, sinm_ref, gmoba_ref, gcq_ref,
                 gckv_ref, wuq_ref, wuk_ref, wuvt_ref, gmla_ref,
                 mq_ref, mk_ref, mvt_ref, lq_ref, lk_ref, lvt_ref, sq_ref, sk_ref, svt_ref,
                 z_ref, xbc_ref, dt_ref):
    x = x_ref[...]
    ms = jnp.mean(x * x, axis=-1, keepdims=True)
    h = (x * lax.rsqrt(ms + EPS) * g_ref[...]).astype(BF16)

    def proj(lo, width):
        return jnp.dot(h, w_ref[:, lo:lo + width], preferred_element_type=F32)

    mvt_ref[...] = _nt_dot(wvt_ref[0], h).astype(BF16)
    svt_ref[...] = _nt_dot(wvt_ref[1], h).astype(BF16)

    cosa = jnp.concatenate([cosa_ref[...]] * 2, axis=1)
    sina = jnp.concatenate([sina_ref[...]] * 2, axis=1)
    for idx, dst in ((0, mq_ref), (1, mk_ref)):
        t = proj(C_MOBA + idx * GROUP_WIDTH, GROUP_WIDTH)
        t = t * lax.rsqrt(_group_sumsq(t, HEAD_DIM) * (1.0 / HEAD_DIM) + EPS) * gmoba_ref[idx:idx + 1, :]
        t = t * cosa + _swap_halves(t, HEAD_DIM, HEAD_DIM // 2, 0) * sina
        dst[...] = t.astype(BF16)

    cq = proj(C_CQ, 256)
    cq = cq * lax.rsqrt(jnp.sum(cq * cq, axis=-1, keepdims=True) * (1.0 / MLA_Q_RANK) + EPS) * gcq_ref[...]
    ckv = proj(C_CKV, LANES)
    ckv = (ckv * lax.rsqrt(jnp.mean(ckv * ckv, axis=-1, keepdims=True) + EPS) * gckv_ref[...]).astype(BF16)
    krdt = proj(C_KR, LANES)
    dt_ref[...] = krdt
    lane = lax.broadcasted_iota(jnp.int32, krdt.shape, 1)
    kr = jnp.where((lane >= KR_LANE) & (lane < KR_LANE + MLA_ROPE), krdt, 0.0)
    cosm = jnp.concatenate([cosm_ref[...]] * N_HEADS, axis=1)
    sinm = jnp.concatenate([sinm_ref[...]] * N_HEADS, axis=1)
    ql = jnp.dot(cq.astype(BF16), wuq_ref[...], preferred_element_type=F32)
    kl = jnp.dot(ckv, wuk_ref[...], preferred_element_type=F32) + jnp.concatenate([kr] * N_HEADS, axis=1)
    for idx, t, dst in ((0, ql, lq_ref), (1, kl, lk_ref)):
        t = t * lax.rsqrt(_group_sumsq(t, LANES) * (1.0 / MLA_QK) + EPS) * gmla_ref[idx:idx + 1, :]
        t = t * cosm + _swap_halves(t, LANES, MLA_ROPE // 2, MLA_NOPE) * sinm
        dst[...] = t.astype(BF16)
    lvt_ref[...] = _nt_dot(wuvt_ref[...], ckv).astype(BF16)

    sq_ref[...] = (proj(C_SB, GROUP_WIDTH) * (HEAD_DIM ** -0.5)).astype(BF16)
    sk_ref[...] = proj(C_SB + GROUP_WIDTH, GROUP_WIDTH).astype(BF16)

    z_ref[...] = proj(C_Z, GROUP_WIDTH)
    xbc_ref[...] = proj(C_XBC, SSM_XBC)


def _inproj(x2d, prm):
    t = x2d.shape[0]
    tm = TM_PROJ
    row = lambda w: pl.BlockSpec((tm, w), lambda i: (i, 0))
    tok = lambda w, dt: (jax.ShapeDtypeStruct((t, w), dt), row(w))
    tr = (jax.ShapeDtypeStruct((GROUP_WIDTH, t), BF16), pl.BlockSpec((GROUP_WIDTH, tm), lambda i: (0, i)))
    outs = [tok(256, BF16), tok(256, BF16), tr, tok(512, BF16), tok(512, BF16), tr,
            tok(256, BF16), tok(256, BF16), tr, tok(256, F32), tok(512, F32), tok(128, F32)]
    out_shapes = [o[0] for o in outs]
    out_specs = [o[1] for o in outs]
    consts = [prm["mix_g"], prm["w_in"], prm["w_vt"]]
    tables = [prm["cosa"], prm["sina"], prm["cosm"], prm["sinm"]]
    tail = [prm["gmoba"], prm["gcq"], prm["gckv"], prm["wuq"], prm["wuk"], prm["wuvt"], prm["gmla"]]
    in_specs = ([row(D_MODEL)] + [_full(a.shape) for a in consts] + [row(LANES)] * 4
                + [_full(a.shape) for a in tail])
    return pl.pallas_call(
        _inproj_body, grid=(t // tm,), in_specs=in_specs, out_specs=out_specs, out_shape=out_shapes,
        compiler_params=_params("arbitrary"), name="inproj",
    )(x2d, *consts, *tables, *tail)


QW = N_HEADS * TQ


def _head_slices(a):
    return [a[:, hd * TQ:(hd + 1) * TQ] for hd in range(N_HEADS)]


def _stack_heads(q):
    head_of_lane = lax.broadcasted_iota(jnp.int32, q.shape, 1) >> 6
    return jnp.concatenate([jnp.where(head_of_lane == hd, q, jnp.zeros_like(q)) for hd in range(N_HEADS)],
                           axis=0)


def _pv(vt_ref, k0, p):
    ps = _head_slices(p.astype(BF16))
    return [jnp.dot(vt_ref[hd * HEAD_DIM:(hd + 1) * HEAD_DIM, pl.ds(k0, TQ)], ps[hd],
                    preferred_element_type=F32) for hd in range(N_HEADS)]


def _key_query_iotas():
    key = lax.broadcasted_iota(jnp.int32, (TQ, QW), 0)
    qry = lax.broadcasted_iota(jnp.int32, (TQ, QW), 1) & (TQ - 1)
    return key, qry


def _softmax_diag(s, vt_ref, k0):
    m_run = jnp.max(s, axis=0, keepdims=True)
    p = jnp.exp(s - m_run)
    return m_run, jnp.sum(p, axis=0, keepdims=True), _pv(vt_ref, k0, p)


def _softmax_step(s, vt_ref, k0, m_run, l_run, acc):
    m_new = jnp.maximum(m_run, jnp.max(s, axis=0, keepdims=True))
    alpha = jnp.exp(m_run - m_new)
    p = jnp.exp(s - m_new)
    l_new = alpha * l_run + jnp.sum(p, axis=0, keepdims=True)
    al = _head_slices(alpha)
    acc = [a * al[hd] + pv for hd, (a, pv) in enumerate(zip(acc, _pv(vt_ref, k0, p)))]
    return m_new, l_new, acc


def _write_heads(o_ref, acc, inv_l, gain_ref):
    outs = []
    for hd in range(N_HEADS):
        o = acc[hd] if inv_l is None else acc[hd] * inv_l[hd]
        ms = jnp.mean(o * o, axis=0, keepdims=True)
        outs.append(o * lax.rsqrt(ms + EPS) * gain_ref[hd * HEAD_DIM:(hd + 1) * HEAD_DIM, :])
    o_ref[0] = jnp.concatenate(outs, axis=0).T.astype(BF16)


def _moba_body(q_ref, k_ref, vt_ref, gout_ref, o_ref, kmean_ref, sel_ref):
    i = pl.program_id(1)
    nblk = SEQ // MOBA_BLOCK

    @pl.when(i == 0)
    def _():
        kmean_ref[...] = jnp.zeros(kmean_ref.shape, F32)
        for n in range(nblk):
            kb = k_ref[0, n * MOBA_BLOCK:(n + 1) * MOBA_BLOCK, :].astype(F32)
            kmean_ref[n:n + 1, :] = jnp.mean(kb, axis=0, keepdims=True)

    qs = _stack_heads(q_ref[0])
    key, qry = _key_query_iotas()

    km = kmean_ref[...]
    km_hi = km.astype(BF16)
    km_lo = (km - km_hi.astype(F32)).astype(BF16)
    gate = _nt_dot(km_hi, qs) + _nt_dot(km_lo, qs)
    blk = lax.broadcasted_iota(jnp.int32, gate.shape, 0)
    rank = jnp.zeros(gate.shape, F32)
    for m in range(nblk - 1):
        gm = gate[m:m + 1, :]
        beats = (gm > gate) | ((gm == gate) & (blk > m))
        rank = rank + jnp.where(beats & (i > m), 1.0, 0.0)
    sel_ref[...] = jnp.where((blk < i) & (rank < float(MOBA_TOPK)), 1.0, 0.0)

    d0 = pl.multiple_of(i * TQ, TQ)
    s = _nt_dot(k_ref[0, pl.ds(d0, TQ), :], qs)
    s = jnp.where(key <= qry, s, MASK_VALUE)
    m_run, l_run, acc = _softmax_diag(s, vt_ref, d0)

    def body(j, carry):
        m_run, l_run, acc = carry
        j0 = pl.multiple_of(j * TQ, TQ)
        s = jnp.where(sel_ref[pl.ds(j, 1), :] > 0.5, _nt_dot(k_ref[0, pl.ds(j0, TQ), :], qs), MASK_VALUE)
        return _softmax_step(s, vt_ref, j0, m_run, l_run, acc)

    m_run, l_run, acc = lax.fori_loop(0, i, body, (m_run, l_run, acc))
    _write_heads(o_ref, acc, _head_slices(1.0 / l_run), gout_ref)


def _attn_call(body, name, q, k, vt, extra, scratch):
    b, _, qw = q.shape
    kw = k.shape[-1]
    return pl.pallas_call(
        body, grid=(b, SEQ // TQ),
        in_specs=[pl.BlockSpec((1, TQ, qw), lambda bi, i: (bi, i, 0)),
                  pl.BlockSpec((1, SEQ, kw), lambda bi, i: (bi, 0, 0)),
                  pl.BlockSpec((GROUP_WIDTH, SEQ), lambda bi, i: (0, bi))]
                 + [_full(a.shape) for a in extra],
        out_specs=pl.BlockSpec((1, TQ, GROUP_WIDTH), lambda bi, i: (bi, i, 0)),
        out_shape=jax.ShapeDtypeStruct((b, SEQ, GROUP_WIDTH), BF16),
        scratch_shapes=scratch,
        compiler_params=_params("arbitrary", "arbitrary"), name=name,
    )(q, k, vt, *extra)


def _moba(q, k, vt, gout):
    scratch = [pltpu.VMEM((16, 256), F32), pltpu.VMEM((16, QW), F32)]
    return _attn_call(_moba_body, "moba", q, k, vt, [gout], scratch)


def _mla_body(q_ref, k_ref, vt_ref, gout_ref, o_ref):
    i = pl.program_id(1)
    key, qry = _key_query_iotas()
    qh = [q_ref[0, :, hd * LANES:(hd + 1) * LANES] for hd in range(N_HEADS)]

    def scores(k0):
        return jnp.concatenate(
            [_nt_dot(k_ref[0, pl.ds(k0, TQ), hd * LANES:(hd + 1) * LANES], qh[hd]) for hd in range(N_HEADS)],
            axis=1)

    d0 = pl.multiple_of(i * TQ, TQ)
    s = jnp.where(key <= qry, scores(d0), MASK_VALUE)
    m_run, l_run, acc = _softmax_diag(s, vt_ref, d0)

    def body(j, carry):
        m_run, l_run, acc = carry
        j0 = pl.multiple_of(j * TQ, TQ)
        return _softmax_step(scores(j0), vt_ref, j0, m_run, l_run, acc)

    m_run, l_run, acc = lax.fori_loop(0, i, body, (m_run, l_run, acc))
    _write_heads(o_ref, acc, _head_slices(1.0 / l_run), gout_ref)


def _mla(q, k, vt, gout):
    return _attn_call(_mla_body, "mla", q, k, vt, [gout], [])


def _sb_body(q_ref, k_ref, vt_ref, uu_ref, gout_ref, o_ref):
    i = pl.program_id(1)
    key, qry = _key_query_iotas()
    strict = key < qry
    qs = _stack_heads(q_ref[0])
    uu = uu_ref[...]

    def logs(z):
        sp = _softplus_neg_abs(z)
        return jnp.minimum(z, 0.0) - sp, -jnp.maximum(z, 0.0) - sp

    def suffix_sum(l1):
        hi = l1.astype(BF16)
        lo = (l1 - hi.astype(F32)).astype(BF16)
        return jnp.dot(uu, jnp.concatenate([hi, lo], axis=0), preferred_element_type=F32)

    d0 = pl.multiple_of(i * TQ, TQ)
    lb, l1 = logs(_nt_dot(k_ref[0, pl.ds(d0, TQ), :], qs))
    l1 = jnp.where(strict, l1, 0.0)
    a = jnp.where(strict, jnp.exp(lb + suffix_sum(l1)), 0.0)
    acc = _pv(vt_ref, d0, a)
    csum = jnp.sum(l1, axis=0, keepdims=True)

    def body(jj, carry):
        csum, acc = carry
        j0 = pl.multiple_of((i - 1 - jj) * TQ, TQ)
        lb, l1 = logs(_nt_dot(k_ref[0, pl.ds(j0, TQ), :], qs))
        a = jnp.exp(lb + (suffix_sum(l1) + csum))
        acc = [x + y for x, y in zip(acc, _pv(vt_ref, j0, a))]
        return csum + jnp.sum(l1, axis=0, keepdims=True), acc

    csum, acc = lax.fori_loop(0, i, body, (csum, acc))
    _write_heads(o_ref, acc, None, gout_ref)


def _sb(q, k, vt, gout):
    r = lax.broadcasted_iota(jnp.int32, (TQ, 2 * TQ), 0)
    c = lax.broadcasted_iota(jnp.int32, (TQ, 2 * TQ), 1) & (TQ - 1)
    uu = jnp.where(c > r, 1.0, 0.0).astype(BF16)
    return _attn_call(_sb_body, "stickbreak", q, k, vt, [uu, gout], [])


def _ssd_body(z_ref, xbc_ref, dt_ref, cw_ref, cb_ref, dtb_ref, alog_ref, dskip_ref, ng_ref, o_ref,
              halo_ref, st_ref):
    c = pl.program_id(1)
    cl = SSM_CHUNK

    @pl.when(c == 0)
    def _():
        halo_ref[...] = jnp.zeros(halo_ref.shape, F32)
        st_ref[...] = jnp.zeros(st_ref.shape, F32)

    xb = xbc_ref[0]
    ext = jnp.concatenate([halo_ref[...], xb], axis=0)
    u = cb_ref[...] + cw_ref[SSM_CONV - 1:SSM_CONV, :] * xb
    for tap in range(SSM_CONV - 1):
        shift = SSM_CONV - 1 - tap
        u = u + cw_ref[tap:tap + 1, :] * pltpu.roll(ext, shift, 0)[8:8 + cl]
    halo_ref[...] = xb[cl - 8:cl]
    xc = u * _sigmoid(u)
    xs = xc[:, 0:GROUP_WIDTH]
    bm = xc[:, GROUP_WIDTH:GROUP_WIDTH + LANES]
    cm = xc[:, GROUP_WIDTH + LANES:SSM_XBC]

    lane = lax.broadcasted_iota(jnp.int32, (cl, LANES), 1)
    head_lane = (lane >= DT_LANE) & (lane < DT_LANE + SSM_HEADS)
    dtv = dt_ref[0] + dtb_ref[...]
    dt = jnp.where(head_lane, jnp.maximum(dtv, 0.0) + _softplus_neg_abs(dtv), 0.0)
    a = dt * (-jnp.exp(alog_ref[...]))

    r = lax.broadcasted_iota(jnp.int32, (cl, cl), 0)
    s = lax.broadcasted_iota(jnp.int32, (cl, cl), 1)
    tril = s <= r
    tril_b = jnp.where(tril, 1.0, 0.0).astype(BF16)
    a1 = a.astype(BF16)
    a2 = (a - a1.astype(F32)).astype(BF16)
    a3 = (a - a1.astype(F32) - a2.astype(F32)).astype(BF16)
    acum = (jnp.dot(tril_b, a1, preferred_element_type=F32)
            + (jnp.dot(tril_b, a2, preferred_element_type=F32)
               + jnp.dot(tril_b, a3, preferred_element_type=F32)))
    acum_t = acum.T

    lane2 = lax.broadcasted_iota(jnp.int32, (cl, GROUP_WIDTH), 1)
    head2 = lane2 >> 6

    def per_head(v):
        out = jnp.broadcast_to(v[:, DT_LANE + 3:DT_LANE + 4], (cl, GROUP_WIDTH))
        for hd in (2, 1, 0):
            out = jnp.where(head2 == hd, v[:, DT_LANE + hd:DT_LANE + hd + 1], out)
        return out

    dt_full = per_head(dt)
    acum_full = per_head(acum)
    last_full = acum_full[cl - 1:cl, :]
    xdt = xs * dt_full
    xdt_b = xdt.astype(BF16)

    grp = lane >> 6
    bm_b = bm.astype(BF16)
    y = jnp.zeros((cl, GROUP_WIDTH), F32)
    for g in range(2):
        cg = jnp.where(grp == g, cm, 0.0).astype(BF16)
        gmat = _nt_dot(cg, bm_b)
        for hd in (2 * g, 2 * g + 1):
            seg = acum[:, DT_LANE + hd:DT_LANE + hd + 1] - acum_t[DT_LANE + hd:DT_LANE + hd + 1, :]
            lmat = jnp.where(tril, jnp.exp(jnp.where(tril, seg, 0.0)), 0.0)
            yd = jnp.dot((gmat * lmat).astype(BF16), xdt_b, preferred_element_type=F32)
            y = jnp.where(head2 == hd, yd, y)

    st_prev = st_ref[...]
    y_off = jnp.exp(acum_full) * jnp.dot(cm.astype(BF16), st_prev.astype(BF16), preferred_element_type=F32)
    dec = jnp.exp(last_full - acum_full)
    s_new = jnp.dot(bm.T.astype(BF16), (xdt * dec).astype(BF16), preferred_element_type=F32)
    rs = lax.broadcasted_iota(jnp.int32, (LANES, GROUP_WIDTH), 0) >> 6
    cs = lax.broadcasted_iota(jnp.int32, (LANES, GROUP_WIDTH), 1) >> 7
    st_ref[...] = jnp.exp(last_full) * st_prev + jnp.where(rs == cs, s_new, 0.0)

    y = y + y_off + dskip_ref[...] * xs
    zz = z_ref[0]
    y = y * (zz * _sigmoid(zz))
    ms = jnp.mean(y * y, axis=-1, keepdims=True)
    o_ref[0] = (y * lax.rsqrt(ms + EPS) * ng_ref[...]).astype(BF16)


def _ssd(z, xbc, dt, prm):
    b = z.shape[0]
    cl = SSM_CHUNK
    small = [prm["ssm_cw"], prm["ssm_cb"], prm["ssm_dtb"], prm["ssm_alog"], prm["ssm_d"], prm["ssm_ng"]]
    return pl.pallas_call(
        _ssd_body, grid=(b, SEQ // cl),
        in_specs=[pl.BlockSpec((1, cl, 256), lambda bi, c: (bi, c, 0)),
                  pl.BlockSpec((1, cl, 512), lambda bi, c: (bi, c, 0)),
                  pl.BlockSpec((1, cl, LANES), lambda bi, c: (bi, c, 0))]
                 + [_full(a.shape) for a in small],
        out_specs=pl.BlockSpec((1, cl, 256), lambda bi, c: (bi, c, 0)),
        out_shape=jax.ShapeDtypeStruct((b, SEQ, 256), BF16),
        scratch_shapes=[pltpu.VMEM((8, SSM_XBC), F32), pltpu.VMEM((LANES, GROUP_WIDTH), F32)],
        compiler_params=_params("arbitrary", "arbitrary"), name="ssd",
    )(z, xbc, dt, *small)


def _outproj_body(x_ref, o1_ref, o2_ref, o3_ref, o4_ref, w_ref, y_ref):
    acc = x_ref[...]
    for n, o_ref in enumerate((o1_ref, o2_ref, o3_ref, o4_ref)):
        acc = acc + jnp.dot(o_ref[...], w_ref[n * GROUP_WIDTH:(n + 1) * GROUP_WIDTH, :],
                            preferred_element_type=F32)
    y_ref[...] = acc


def _outproj(x2d, outs, w_out):
    t = x2d.shape[0]
    tm = TM_PROJ
    return pl.pallas_call(
        _outproj_body, grid=(t // tm,),
        in_specs=[pl.BlockSpec((tm, D_MODEL), lambda i: (i, 0))]
                 + [pl.BlockSpec((tm, GROUP_WIDTH), lambda i: (i, 0))] * 4 + [_full(w_out.shape)],
        out_specs=pl.BlockSpec((tm, D_MODEL), lambda i: (i, 0)),
        out_shape=jax.ShapeDtypeStruct((t, D_MODEL), F32),
        compiler_params=_params("arbitrary"), name="outproj",
    )(x2d, *outs, w_out)


def _ffn_body(x_ref, halo_ref, g_ref, win_ref, cw_ref, cb_ref, wout_ref, y_ref, act_ref):
    tm = TM_PROJ
    first = (pl.program_id(0) % (SEQ // tm)) == 0
    g = g_ref[...]

    def norm(v):
        ms = jnp.mean(v * v, axis=-1, keepdims=True)
        return v * lax.rsqrt(ms + EPS) * g

    x = x_ref[...]
    h_halo = jnp.where(first, 0.0, norm(halo_ref[...])).astype(BF16)
    h_ext = jnp.concatenate([h_halo, norm(x).astype(BF16)], axis=0)

    def conv(u, lo, width):
        w = cw_ref[:, lo:lo + width]
        out = cb_ref[:, lo:lo + width] + w[2:3, :] * u[FFN_HALO:, :]
        out = out + w[1:2, :] * pltpu.roll(u, 1, 0)[FFN_HALO:, :]
        return out + w[0:1, :] * pltpu.roll(u, 2, 0)[FFN_HALO:, :]

    for lo, width in FFN_CHUNKS:
        gate = conv(jnp.dot(h_ext, win_ref[:, lo:lo + width], preferred_element_type=F32), lo, width)
        up = conv(jnp.dot(h_ext, win_ref[:, D_FF + lo:D_FF + lo + width], preferred_element_type=F32),
                  D_FF + lo, width)
        act_ref[:, lo:lo + width] = (gate * _sigmoid(gate) * up).astype(BF16)
    y_ref[...] = x + jnp.dot(act_ref[...], wout_ref[...], preferred_element_type=F32)


def _ffn(x2d, prm):
    t = x2d.shape[0]
    tm = TM_PROJ
    per = tm // FFN_HALO
    consts = [prm["ffn_g"], prm["ffn_win"], prm["ffn_cw"], prm["ffn_cb"], prm["ffn_wout"]]
    return pl.pallas_call(
        _ffn_body, grid=(t // tm,),
        in_specs=[pl.BlockSpec((tm, D_MODEL), lambda i: (i, 0)),
                  pl.BlockSpec((FFN_HALO, D_MODEL), lambda i: (jnp.maximum(i * per - 1, 0), 0))]
                 + [_full(a.shape) for a in consts],
        out_specs=pl.BlockSpec((tm, D_MODEL), lambda i: (i, 0)),
        out_shape=jax.ShapeDtypeStruct((t, D_MODEL), F32),
        scratch_shapes=[pltpu.VMEM((tm, D_FF), BF16)],
        compiler_params=_params("arbitrary"), name="ffn",
    )(x2d, x2d, *consts)


def _rope_tables(positions):
    pos = positions.reshape(-1).astype(F32)[:, None]
    inv_a = 1.0 / (ROPE_THETA ** (jnp.arange(0, HEAD_DIM, 2, dtype=F32) / HEAD_DIM))
    ang_a = pos * inv_a
    cosa = jnp.tile(jnp.cos(ang_a), (1, 4))
    sina = jnp.tile(jnp.concatenate([-jnp.sin(ang_a), jnp.sin(ang_a)], axis=1), (1, 2))
    inv_m = 1.0 / (ROPE_THETA ** (jnp.arange(0, MLA_ROPE, 2, dtype=F32) / MLA_ROPE))
    ang_m = pos * inv_m
    n = pos.shape[0]
    ones = jnp.ones((n, MLA_NOPE), F32)
    zeros32 = jnp.zeros((n, LANES - MLA_QK), F32)
    cosm = jnp.concatenate([ones, jnp.cos(ang_m), jnp.cos(ang_m), 1.0 + zeros32], axis=1)
    sinm = jnp.concatenate([0.0 * ones, -jnp.sin(ang_m), jnp.sin(ang_m), zeros32], axis=1)
    return cosa, sina, cosm, sinm


def _pad_heads(w, per_head, width):
    lead = w.shape[:-1]
    w = w.reshape(*lead, N_HEADS, per_head)
    w = jnp.pad(w, [(0, 0)] * len(lead) + [(0, 0), (0, width - per_head)])
    return w.reshape(*lead, N_HEADS * width)


def _layer_params(l, mix_norm_g, w_in, moba_qk_g, mla_q_norm_g, mla_kv_norm_g, mla_w_uq, mla_w_ukv,
                  mla_qk_g, ssm_conv_w, ssm_conv_b, ssm_dt_bias, ssm_a_log, ssm_d, ssm_norm_g,
                  head_out_g, w_out, ffn_norm_g, ffn_w_in, ffn_conv_w, ffn_conv_b, ffn_w_out):
    w = w_in[l]
    o = 0
    moba_w = w[:, o:o + 768]; o += 768
    cq_w = w[:, o:o + MLA_Q_RANK]; o += MLA_Q_RANK
    ckv_w = w[:, o:o + MLA_KV_RANK]; o += MLA_KV_RANK
    kr_w = w[:, o:o + MLA_ROPE]; o += MLA_ROPE
    sb_w = w[:, o:o + 768]; o += 768
    z_w = w[:, o:o + 256]; o += 256
    xbc_w = w[:, o:o + SSM_XBC]; o += SSM_XBC
    dt_w = w[:, o:o + SSM_HEADS]
    zc = lambda n: jnp.zeros((D_MODEL, n), F32)
    w_r = jnp.concatenate([
        moba_w[:, :512], cq_w, zc(256 - MLA_Q_RANK), ckv_w,
        zc(KR_LANE), kr_w, dt_w, zc(LANES - DT_LANE - SSM_HEADS),
        sb_w[:, :512], z_w, xbc_w], axis=1).astype(BF16)
    w_vt = jnp.stack([moba_w[:, 512:].T, sb_w[:, 512:].T]).astype(BF16)

    lane_row = lambda v, start: jnp.zeros((1, LANES), F32).at[0, start:start + v.shape[0]].set(v)
    moba_scale = jnp.array([[HEAD_DIM ** -0.5], [1.0]], F32)
    mla_scale = jnp.array([[MLA_QK ** -0.5], [1.0]], F32)
    ukv = mla_w_ukv[l].reshape(MLA_KV_RANK, N_HEADS, MLA_NOPE + HEAD_DIM)
    hog = head_out_g[l].reshape(3, GROUP_WIDTH, 1)
    return {
        "mix_g": mix_norm_g[l][None, :], "w_in": w_r, "w_vt": w_vt,
        "gmoba": jnp.tile(moba_qk_g[l], (1, N_HEADS)) * moba_scale,
        "gcq": jnp.pad(mla_q_norm_g[l], (0, 256 - MLA_Q_RANK))[None, :],
        "gckv": mla_kv_norm_g[l][None, :],
        "wuq": jnp.pad(_pad_heads(mla_w_uq[l], MLA_QK, LANES), ((0, 256 - MLA_Q_RANK), (0, 0))).astype(BF16),
        "wuk": _pad_heads(ukv[:, :, :MLA_NOPE].reshape(MLA_KV_RANK, -1), MLA_NOPE, LANES).astype(BF16),
        "wuvt": ukv[:, :, MLA_NOPE:].reshape(MLA_KV_RANK, GROUP_WIDTH).T.astype(BF16),
        "gmla": jnp.tile(jnp.pad(mla_qk_g[l], ((0, 0), (0, LANES - MLA_QK))), (1, N_HEADS)) * mla_scale,
        "ssm_cw": ssm_conv_w[l], "ssm_cb": ssm_conv_b[l][None, :],
        "ssm_dtb": lane_row(ssm_dt_bias[l], DT_LANE), "ssm_alog": lane_row(ssm_a_log[l], DT_LANE),
        "ssm_d": jnp.repeat(ssm_d[l], HEAD_DIM)[None, :], "ssm_ng": ssm_norm_g[l][None, :],
        "g_moba_out": hog[0], "g_mla_out": hog[1], "g_sb_out": hog[2],
        "w_out": w_out[l].astype(BF16),
        "ffn_g": ffn_norm_g[l][None, :], "ffn_win": ffn_w_in[l].astype(BF16),
        "ffn_cw": ffn_conv_w[l], "ffn_cb": ffn_conv_b[l][None, :], "ffn_wout": ffn_w_out[l].astype(BF16),
    }


def _layer(x2d, bsz, prm):
    (mq, mk, mvt, lq, lk, lvt, sq, sk, svt, z, xbc, dt) = _inproj(x2d, prm)
    r3 = lambda a: a.reshape(bsz, SEQ, a.shape[-1])
    o_moba = _moba(r3(mq), r3(mk), mvt, prm["g_moba_out"])
    o_mla = _mla(r3(lq), r3(lk), lvt, prm["g_mla_out"])
    o_sb = _sb(r3(sq), r3(sk), svt, prm["g_sb_out"])
    o_ssm = _ssd(r3(z), r3(xbc), r3(dt), prm)
    flat = lambda a: a.reshape(bsz * SEQ, GROUP_WIDTH)
    x2d = _outproj(x2d, [flat(o_moba), flat(o_mla), flat(o_sb), flat(o_ssm)], prm["w_out"])
    return _ffn(x2d, prm)


def kernel(x, positions, mix_norm_g, w_in, moba_qk_g, mla_q_norm_g, mla_kv_norm_g, mla_w_uq, mla_w_ukv,
           mla_qk_g, ssm_conv_w, ssm_conv_b, ssm_dt_bias, ssm_a_log, ssm_d, ssm_norm_g, head_out_g, w_out,
           ffn_norm_g, ffn_w_in, ffn_conv_w, ffn_conv_b, ffn_w_out):
    bsz, seq, d = x.shape
    assert seq == SEQ and d == D_MODEL
    cosa, sina, cosm, sinm = _rope_tables(positions)
    x2d = x.reshape(bsz * seq, d)
    for l in range(w_in.shape[0]):
        prm = _layer_params(l, mix_norm_g, w_in, moba_qk_g, mla_q_norm_g, mla_kv_norm_g, mla_w_uq,
                            mla_w_ukv, mla_qk_g, ssm_conv_w, ssm_conv_b, ssm_dt_bias, ssm_a_log, ssm_d,
                            ssm_norm_g, head_out_g, w_out, ffn_norm_g, ffn_w_in, ffn_conv_w, ffn_conv_b,
                            ffn_w_out)
        prm.update(cosa=cosa, sina=sina, cosm=cosm, sinm=sinm)
        x2d = _layer(x2d, bsz, prm)
    return x2d.reshape(bsz, seq, d)
```

```python
import functools
import math

import jax
import jax.numpy as jnp
from jax import lax
from jax.experimental import pallas as pl
from jax.experimental.pallas import tpu as pltpu

F32 = jnp.float32
BF16 = jnp.bfloat16

D_MODEL = 1024
SEQ = 2048
GROUP_WIDTH = 256
HEAD_DIM = 64
N_HEADS = 4
MOBA_BLOCK = 256
MOBA_TOPK = 3
MLA_Q_RANK = 192
MLA_KV_RANK = 128
MLA_NOPE = 64
MLA_ROPE = 32
MLA_QK = MLA_NOPE + MLA_ROPE
SSM_HEADS = 4
SSM_STATE = 64
SSM_CONV = 4
SSM_CHUNK = 128
SSM_XBC = 512
D_FF = 2816
FFN_CONV = 3
ROPE_THETA = 10000.0
EPS = 1e-6

LANES = 128
VMEM_LIMIT_BYTES = 56 * 1024 * 1024
MASK_VALUE = -1e30
LOG2E = math.log2(math.e)

C_MOBA = 0
C_CQ = 1024
C_CKV = 1280
C_KR = 1408
C_KRS = 1536
C_SB = 1664
C_Z = 2176
C_XBC = 2432
KR_LANE = 64
DT_LANE = 96

TM_PROJ = 512
TQ = 256
FFN_HALO = 16
FFN_CHUNKS = ((0, 512), (512, 512), (1024, 512), (1536, 512), (2048, 512), (2560, 256))


def _params(*sem):
    return pltpu.CompilerParams(dimension_semantics=sem, vmem_limit_bytes=VMEM_LIMIT_BYTES)


def _full(shape):
    n = len(shape)
    return pl.BlockSpec(shape, lambda *_: (0,) * n)


def _sigmoid(x):
    return 1.0 / (1.0 + jnp.exp(-x))


def _softplus_neg_abs(x):
    return jnp.log(1.0 + jnp.exp(-jnp.abs(x)))


def _group_sumsq(x, group):
    xx = x * x
    outs = []
    for c in range(x.shape[1] // LANES):
        slab = xx[:, c * LANES:(c + 1) * LANES]
        if group == LANES:
            s = jnp.sum(slab, axis=1, keepdims=True)
            outs.append(jnp.broadcast_to(s, slab.shape))
        else:
            lane = lax.broadcasted_iota(jnp.int32, slab.shape, 1)
            lo = lane < group
            s_lo = jnp.sum(jnp.where(lo, slab, 0.0), axis=1, keepdims=True)
            s_hi = jnp.sum(jnp.where(lo, 0.0, slab), axis=1, keepdims=True)
            outs.append(jnp.where(lo, s_lo, s_hi))
    return outs[0] if len(outs) == 1 else jnp.concatenate(outs, axis=1)


def _nt_dot(a, b):
    return lax.dot_general(a, b, (((1,), (1,)), ((), ())), preferred_element_type=F32)


def _inproj_body(x_ref, g_ref, w_ref, wvt_ref, cosa_ref, sina_ref, cosm_ref, sinm_ref, gmoba_ref, gcq_ref,
                 gckv_ref, wuq_ref, wuk_ref, wuvt_ref, gmla_ref,
                 mq_ref, mk_ref, mvt_ref, lq_ref, lk_ref, lvt_ref, sq_ref, sk_ref, svt_ref,
                 z_ref, xbc_ref, dt_ref):
    x = x_ref[...]
    ms = jnp.mean(x * x, axis=-1, keepdims=True)
    h = (x * lax.rsqrt(ms + EPS) * g_ref[...]).astype(BF16)

    def proj(lo, width):
        return jnp.dot(h, w_ref[:, lo:lo + width], preferred_element_type=F32)

    mvt_ref[...] = _nt_dot(wvt_ref[0], h).astype(BF16)
    svt_ref[...] = _nt_dot(wvt_ref[1], h).astype(BF16)

    def norm_rope(t, t_sw, group, n, gain_ref, idx, cos, sin):
        r = lax.rsqrt(_group_sumsq(t, group) * (1.0 / n) + EPS)
        return (r * ((t * gain_ref[idx:idx + 1, :]) * cos + (t_sw * gain_ref[idx + 2:idx + 3, :]) * sin)).astype(BF16)

    cosa = jnp.concatenate([cosa_ref[...]] * 2, axis=1)
    sina = jnp.concatenate([sina_ref[...]] * 2, axis=1)
    for idx, dst in ((0, mq_ref), (1, mk_ref)):
        t = proj(C_MOBA + idx * GROUP_WIDTH, GROUP_WIDTH)
        t_sw = proj(C_MOBA + (idx + 2) * GROUP_WIDTH, GROUP_WIDTH)
        dst[...] = norm_rope(t, t_sw, HEAD_DIM, HEAD_DIM, gmoba_ref, idx, cosa, sina)

    cq = proj(C_CQ, 256)
    cq = (cq * lax.rsqrt(jnp.sum(cq * cq, axis=-1, keepdims=True) * (1.0 / MLA_Q_RANK) + EPS)
          * gcq_ref[...]).astype(BF16)
    ckv = proj(C_CKV, LANES)
    ckv = (ckv * lax.rsqrt(jnp.mean(ckv * ckv, axis=-1, keepdims=True) + EPS) * gckv_ref[...]).astype(BF16)
    krdt = proj(C_KR, LANES)
    dt_ref[...] = krdt
    lane = lax.broadcasted_iota(jnp.int32, krdt.shape, 1)
    kr = jnp.where((lane >= KR_LANE) & (lane < KR_LANE + MLA_ROPE), krdt, 0.0)
    cosm = jnp.concatenate([cosm_ref[...]] * N_HEADS, axis=1)
    sinm = jnp.concatenate([sinm_ref[...]] * N_HEADS, axis=1)
    ql = jnp.dot(cq, wuq_ref[0], preferred_element_type=F32)
    ql_sw = jnp.dot(cq, wuq_ref[1], preferred_element_type=F32)
    kl = jnp.dot(ckv, wuk_ref[...], preferred_element_type=F32) + jnp.concatenate([kr] * N_HEADS, axis=1)
    kl_sw = jnp.concatenate([proj(C_KRS, LANES)] * N_HEADS, axis=1)
    lq_ref[...] = norm_rope(ql, ql_sw, LANES, MLA_QK, gmla_ref, 0, cosm, sinm)
    lk_ref[...] = norm_rope(kl, kl_sw, LANES, MLA_QK, gmla_ref, 1, cosm, sinm)
    lvt_ref[...] = _nt_dot(wuvt_ref[...], ckv).astype(BF16)

    sq_ref[...] = (proj(C_SB, GROUP_WIDTH) * (HEAD_DIM ** -0.5)).astype(BF16)
    sk_ref[...] = proj(C_SB + GROUP_WIDTH, GROUP_WIDTH).astype(BF16)

    z_ref[...] = proj(C_Z, GROUP_WIDTH)
    xbc_ref[...] = proj(C_XBC, SSM_XBC)


def _inproj(x2d, prm):
    t = x2d.shape[0]
    tm = TM_PROJ
    row = lambda w: pl.BlockSpec((tm, w), lambda i: (i, 0))
    tok = lambda w, dt: (jax.ShapeDtypeStruct((t, w), dt), row(w))
    tr = (jax.ShapeDtypeStruct((GROUP_WIDTH, t), BF16), pl.BlockSpec((GROUP_WIDTH, tm), lambda i: (0, i)))
    outs = [tok(256, BF16), tok(256, BF16), tr, tok(512, BF16), tok(512, BF16), tr,
            tok(256, BF16), tok(256, BF16), tr, tok(256, F32), tok(512, F32), tok(128, F32)]
    out_shapes = [o[0] for o in outs]
    out_specs = [o[1] for o in outs]
    consts = [prm["mix_g"], prm["w_in"], prm["w_vt"]]
    tables = [prm["cosa"], prm["sina"], prm["cosm"], prm["sinm"]]
    tail = [prm["gmoba"], prm["gcq"], prm["gckv"], prm["wuq"], prm["wuk"], prm["wuvt"], prm["gmla"]]
    in_specs = ([row(D_MODEL)] + [_full(a.shape) for a in consts] + [row(LANES)] * 4
                + [_full(a.shape) for a in tail])
    return pl.pallas_call(
        _inproj_body, grid=(t // tm,), in_specs=in_specs, out_specs=out_specs, out_shape=out_shapes,
        compiler_params=_params("arbitrary"), name="inproj",
    )(x2d, *consts, *tables, *tail)


QW = N_HEADS * TQ


def _head_slices(a):
    return [a[:, hd * TQ:(hd + 1) * TQ] for hd in range(N_HEADS)]


def _stack_heads(q):
    head_of_lane = lax.broadcasted_iota(jnp.int32, q.shape, 1) >> 6
    return jnp.concatenate([jnp.where(head_of_lane == hd, q, jnp.zeros_like(q)) for hd in range(N_HEADS)],
                           axis=0)


ONES_ROWS = 16


def _pv(vt_ref, blk, p, with_sum):
    k0 = pl.multiple_of(blk * TQ, TQ)
    ps = _head_slices(p.astype(BF16))
    outs = []
    for hd in range(N_HEADS):
        lhs = vt_ref[hd * HEAD_DIM:(hd + 1) * HEAD_DIM, pl.ds(k0, TQ)]
        if with_sum:
            r = lax.broadcasted_iota(jnp.int32, (ONES_ROWS, TQ), 0)
            lhs = jnp.concatenate([lhs, jnp.where(r == 0, 1.0, 0.0).astype(BF16)], axis=0)
        outs.append(jnp.dot(lhs, ps[hd], preferred_element_type=F32))
    return outs


def _key_query_iotas():
    key = lax.broadcasted_iota(jnp.int32, (TQ, QW), 0)
    qry = lax.broadcasted_iota(jnp.int32, (TQ, QW), 1) & (TQ - 1)
    return key, qry


def _softmax_init():
    return (jnp.full((1, QW), MASK_VALUE, F32), [jnp.zeros((HEAD_DIM + ONES_ROWS, TQ), F32)] * N_HEADS)


def _softmax_step(s, vt_ref, blk, m_run, acc, keep=None):
    m_new = jnp.maximum(m_run, jnp.max(s, axis=0, keepdims=True))
    m_sub = m_new
    if keep is not None:
        m_new = jnp.where(keep, m_new, m_run)
        m_sub = jnp.where(keep, m_new, -MASK_VALUE)
    al = _head_slices(jnp.exp2(m_run - m_new))
    pv = _pv(vt_ref, blk, jnp.exp2(s - m_sub), True)
    return m_new, [a * al[hd] + x for hd, (a, x) in enumerate(zip(acc, pv))]


def _pipelined(i, s_a, s_b, block_of, scores, step, carry, last=None):
    def fetch(n):
        return scores(jnp.clip(block_of(n), 0, i))

    s_a[...] = fetch(0)

    def pair(t, carry):
        n0 = 2 * t
        s_b[...] = fetch(n0 + 1)
        carry = step(s_a[...], block_of(n0), carry)
        s_a[...] = fetch(n0 + 2)
        return step(s_b[...], block_of(n0 + 1), carry)

    carry = lax.fori_loop(0, lax.shift_right_logical(i, 1), pair, carry)

    def tail_even(c):
        return c if last is None else last(s_a[...], c)

    def tail_odd(c):
        if last is not None:
            s_b[...] = fetch(i)
        c = step(s_a[...], block_of(i - 1), c)
        return c if last is None else last(s_b[...], c)

    return lax.cond((i & 1) == 1, tail_odd, tail_even, carry)


def _write_heads(o_ref, acc, gain_ref, normalise):
    outs = []
    for hd in range(N_HEADS):
        o = acc[hd][:HEAD_DIM]
        if normalise:
            o = o * (1.0 / acc[hd][HEAD_DIM:HEAD_DIM + 1])
        ms = jnp.mean(o * o, axis=0, keepdims=True)
        outs.append(o * lax.rsqrt(ms + EPS) * gain_ref[hd * HEAD_DIM:(hd + 1) * HEAD_DIM, :])
    o_ref[0] = jnp.concatenate(outs, axis=0).T.astype(BF16)


def _moba_body(q_ref, k_ref, vt_ref, gout_ref, o_ref, sa_ref, sb_ref, kmean_ref, sel_ref):
    i = pl.program_id(1)
    nblk = SEQ // MOBA_BLOCK

    @pl.when(i == 0)
    def _():
        kmean_ref[...] = jnp.zeros(kmean_ref.shape, F32)
        for n in range(nblk):
            kb = k_ref[0, n * MOBA_BLOCK:(n + 1) * MOBA_BLOCK, :].astype(F32)
            kmean_ref[n:n + 1, :] = jnp.mean(kb, axis=0, keepdims=True)

    qs = _stack_heads(q_ref[0])
    key, qry = _key_query_iotas()

    km = kmean_ref[...]
    km_hi = km.astype(BF16)
    km_lo = (km - km_hi.astype(F32)).astype(BF16)
    gate = _nt_dot(km_hi, qs) + _nt_dot(km_lo, qs)
    blk = lax.broadcasted_iota(jnp.int32, gate.shape, 0)
    rank = jnp.zeros(gate.shape, F32)
    for m in range(nblk - 1):
        gm = gate[m:m + 1, :]
        beats = (gm > gate) | ((gm == gate) & (blk > m))
        rank = rank + jnp.where(beats & (i > m), 1.0, 0.0)
    sel_ref[...] = jnp.where((blk < i) & (rank < float(MOBA_TOPK)), 1.0, 0.0)

    def scores(blk):
        return _nt_dot(k_ref[0, pl.ds(pl.multiple_of(blk * TQ, TQ), TQ), :], qs)

    def step(s, blk, carry):
        return _softmax_step(s, vt_ref, blk, *carry, keep=sel_ref[pl.ds(blk, 1), :] > 0.5)

    def diagonal(s, carry):
        return _softmax_step(jnp.where(key <= qry, s, MASK_VALUE), vt_ref, i, *carry)

    _, acc = _pipelined(i, sa_ref, sb_ref, lambda n: n, scores, step, _softmax_init(), diagonal)
    _write_heads(o_ref, acc, gout_ref, True)


def _attn_call(body, name, q, k, vt, extra, scratch):
    b, _, qw = q.shape
    kw = k.shape[-1]
    return pl.pallas_call(
        body, grid=(b, SEQ // TQ),
        in_specs=[pl.BlockSpec((1, TQ, qw), lambda bi, i: (bi, i, 0)),
                  pl.BlockSpec((1, SEQ, kw), lambda bi, i: (bi, 0, 0)),
                  pl.BlockSpec((GROUP_WIDTH, SEQ), lambda bi, i: (0, bi))]
                 + [_full(a.shape) for a in extra],
        out_specs=pl.BlockSpec((1, TQ, GROUP_WIDTH), lambda bi, i: (bi, i, 0)),
        out_shape=jax.ShapeDtypeStruct((b, SEQ, GROUP_WIDTH), BF16),
        scratch_shapes=[pltpu.VMEM((TQ, QW), F32), pltpu.VMEM((TQ, QW), F32)] + scratch,
        compiler_params=_params("arbitrary", "arbitrary"), name=name,
    )(q, k, vt, *extra)


def _moba(q, k, vt, gout):
    scratch = [pltpu.VMEM((16, 256), F32), pltpu.VMEM((16, QW), F32)]
    return _attn_call(_moba_body, "moba", q, k, vt, [gout], scratch)


def _mla_body(q_ref, k_ref, vt_ref, gout_ref, o_ref, sa_ref, sb_ref):
    i = pl.program_id(1)
    key, qry = _key_query_iotas()
    qh = [q_ref[0, :, hd * LANES:(hd + 1) * LANES] for hd in range(N_HEADS)]

    def scores(blk):
        k0 = pl.multiple_of(blk * TQ, TQ)
        return jnp.concatenate(
            [_nt_dot(k_ref[0, pl.ds(k0, TQ), hd * LANES:(hd + 1) * LANES], qh[hd]) for hd in range(N_HEADS)],
            axis=1)

    def step(s, blk, carry):
        return _softmax_step(s, vt_ref, blk, *carry)

    def diagonal(s, carry):
        return _softmax_step(jnp.where(key <= qry, s, MASK_VALUE), vt_ref, i, *carry)

    _, acc = _pipelined(i, sa_ref, sb_ref, lambda n: n, scores, step, _softmax_init(), diagonal)
    _write_heads(o_ref, acc, gout_ref, True)


def _mla(q, k, vt, gout):
    return _attn_call(_mla_body, "mla", q, k, vt, [gout], [])


def _sb_body(q_ref, k_ref, vt_ref, uu_ref, gout_ref, o_ref, sa_ref, sb_ref):
    i = pl.program_id(1)
    key, qry = _key_query_iotas()
    strict = key < qry
    qs = _stack_heads(q_ref[0])

    def scores(blk):
        return _nt_dot(k_ref[0, pl.ds(pl.multiple_of(blk * TQ, TQ), TQ), :], qs)

    def neg_log_survive(z):
        return jnp.maximum(z, 0.0) + jnp.log(1.0 + jnp.exp2(jnp.abs(z) * (-LOG2E)))

    def suffix_sums(c):
        hi = c.astype(BF16)
        lo = (c - hi.astype(F32)).astype(BF16)
        t = jnp.dot(uu_ref[...], jnp.concatenate([hi, lo], axis=0), preferred_element_type=F32)
        return t[:TQ], t[TQ:TQ + 1]

    def step(z, blk, carry, diagonal=False):
        csum, acc = carry
        c = neg_log_survive(z)
        if diagonal:
            c = jnp.where(strict, c, 0.0)
        incl, total = suffix_sums(c)
        a = jnp.exp(z - (incl + csum))
        if diagonal:
            a = jnp.where(strict, a, 0.0)
        pv = _pv(vt_ref, blk, a, False)
        return csum + total, [x + y for x, y in zip(acc, pv)]

    zero = (jnp.zeros((1, QW), F32), [jnp.zeros((HEAD_DIM, TQ), F32)] * N_HEADS)
    carry = step(scores(i), i, zero, diagonal=True)
    _, acc = _pipelined(i, sa_ref, sb_ref, lambda n: i - 1 - n, scores, step, carry)
    _write_heads(o_ref, acc, gout_ref, False)


SB_SUM_ROWS = TQ + 16


def _sb(q, k, vt, gout):
    r = lax.broadcasted_iota(jnp.int32, (SB_SUM_ROWS, 2 * TQ), 0)
    c = lax.broadcasted_iota(jnp.int32, (SB_SUM_ROWS, 2 * TQ), 1) & (TQ - 1)
    uu = jnp.where((c >= r) | (r == TQ), 1.0, 0.0).astype(BF16)
    return _attn_call(_sb_body, "stickbreak", q, k, vt, [uu, gout], [])


def _ssd_body(z_ref, xbc_ref, dt_ref, cw_ref, cb_ref, dtb_ref, alog_ref, dskip_ref, ng_ref, o_ref,
              halo_ref, st_ref):
    c = pl.program_id(1)
    cl = SSM_CHUNK

    @pl.when(c == 0)
    def _():
        halo_ref[...] = jnp.zeros(halo_ref.shape, F32)
        st_ref[...] = jnp.zeros(st_ref.shape, F32)

    xb = xbc_ref[0]
    ext = jnp.concatenate([halo_ref[...], xb], axis=0)
    u = cb_ref[...] + cw_ref[SSM_CONV - 1:SSM_CONV, :] * xb
    for tap in range(SSM_CONV - 1):
        shift = SSM_CONV - 1 - tap
        u = u + cw_ref[tap:tap + 1, :] * pltpu.roll(ext, shift, 0)[8:8 + cl]
    halo_ref[...] = xb[cl - 8:cl]
    xc = u * _sigmoid(u)
    xs = xc[:, 0:GROUP_WIDTH]
    bm = xc[:, GROUP_WIDTH:GROUP_WIDTH + LANES]
    cm = xc[:, GROUP_WIDTH + LANES:SSM_XBC]

    lane = lax.broadcasted_iota(jnp.int32, (cl, LANES), 1)
    head_lane = (lane >= DT_LANE) & (lane < DT_LANE + SSM_HEADS)
    dtv = dt_ref[0] + dtb_ref[...]
    dt = jnp.where(head_lane, jnp.maximum(dtv, 0.0) + _softplus_neg_abs(dtv), 0.0)
    a = dt * (-jnp.exp(alog_ref[...]))

    r = lax.broadcasted_iota(jnp.int32, (cl, cl), 0)
    s = lax.broadcasted_iota(jnp.int32, (cl, cl), 1)
    tril = s <= r
    tril_b = jnp.where(tril, 1.0, 0.0).astype(BF16)
    a1 = a.astype(BF16)
    a2 = (a - a1.astype(F32)).astype(BF16)
    a3 = (a - a1.astype(F32) - a2.astype(F32)).astype(BF16)
    acum = (jnp.dot(tril_b, a1, preferred_element_type=F32)
            + (jnp.dot(tril_b, a2, preferred_element_type=F32)
               + jnp.dot(tril_b, a3, preferred_element_type=F32)))
    acum_t = acum.T

    lane2 = lax.broadcasted_iota(jnp.int32, (cl, GROUP_WIDTH), 1)
    head2 = lane2 >> 6

    def per_head(v):
        out = jnp.broadcast_to(v[:, DT_LANE + 3:DT_LANE + 4], (cl, GROUP_WIDTH))
        for hd in (2, 1, 0):
            out = jnp.where(head2 == hd, v[:, DT_LANE + hd:DT_LANE + hd + 1], out)
        return out

    dt_full = per_head(dt)
    acum_full = per_head(acum)
    last_full = acum_full[cl - 1:cl, :]
    xdt = xs * dt_full
    xdt_b = xdt.astype(BF16)

    grp = lane >> 6
    bm_b = bm.astype(BF16)
    y = jnp.zeros((cl, GROUP_WIDTH), F32)
    for g in range(2):
        cg = jnp.where(grp == g, cm, 0.0).astype(BF16)
        gmat = _nt_dot(cg, bm_b)
        for hd in (2 * g, 2 * g + 1):
            seg = acum[:, DT_LANE + hd:DT_LANE + hd + 1] - acum_t[DT_LANE + hd:DT_LANE + hd + 1, :]
            lmat = jnp.where(tril, jnp.exp(jnp.where(tril, seg, 0.0)), 0.0)
            yd = jnp.dot((gmat * lmat).astype(BF16), xdt_b, preferred_element_type=F32)
            y = jnp.where(head2 == hd, yd, y)

    st_prev = st_ref[...]
    y_off = jnp.exp(acum_full) * jnp.dot(cm.astype(BF16), st_prev.astype(BF16), preferred_element_type=F32)
    dec = jnp.exp(last_full - acum_full)
    s_new = jnp.dot(bm.T.astype(BF16), (xdt * dec).astype(BF16), preferred_element_type=F32)
    rs = lax.broadcasted_iota(jnp.int32, (LANES, GROUP_WIDTH), 0) >> 6
    cs = lax.broadcasted_iota(jnp.int32, (LANES, GROUP_WIDTH), 1) >> 7
    st_ref[...] = jnp.exp(last_full) * st_prev + jnp.where(rs == cs, s_new, 0.0)

    y = y + y_off + dskip_ref[...] * xs
    zz = z_ref[0]
    y = y * (zz * _sigmoid(zz))
    ms = jnp.mean(y * y, axis=-1, keepdims=True)
    o_ref[0] = (y * lax.rsqrt(ms + EPS) * ng_ref[...]).astype(BF16)


def _ssd(z, xbc, dt, prm):
    b = z.shape[0]
    cl = SSM_CHUNK
    small = [prm["ssm_cw"], prm["ssm_cb"], prm["ssm_dtb"], prm["ssm_alog"], prm["ssm_d"], prm["ssm_ng"]]
    return pl.pallas_call(
        _ssd_body, grid=(b, SEQ // cl),
        in_specs=[pl.BlockSpec((1, cl, 256), lambda bi, c: (bi, c, 0)),
                  pl.BlockSpec((1, cl, 512), lambda bi, c: (bi, c, 0)),
                  pl.BlockSpec((1, cl, LANES), lambda bi, c: (bi, c, 0))]
                 + [_full(a.shape) for a in small],
        out_specs=pl.BlockSpec((1, cl, 256), lambda bi, c: (bi, c, 0)),
        out_shape=jax.ShapeDtypeStruct((b, SEQ, 256), BF16),
        scratch_shapes=[pltpu.VMEM((8, SSM_XBC), F32), pltpu.VMEM((LANES, GROUP_WIDTH), F32)],
        compiler_params=_params("arbitrary", "arbitrary"), name="ssd",
    )(z, xbc, dt, *small)


def _outproj_body(x_ref, o1_ref, o2_ref, o3_ref, o4_ref, w_ref, y_ref):
    acc = x_ref[...]
    for n, o_ref in enumerate((o1_ref, o2_ref, o3_ref, o4_ref)):
        acc = acc + jnp.dot(o_ref[...], w_ref[n * GROUP_WIDTH:(n + 1) * GROUP_WIDTH, :],
                            preferred_element_type=F32)
    y_ref[...] = acc


def _outproj(x2d, outs, w_out):
    t = x2d.shape[0]
    tm = TM_PROJ
    return pl.pallas_call(
        _outproj_body, grid=(t // tm,),
        in_specs=[pl.BlockSpec((tm, D_MODEL), lambda i: (i, 0))]
                 + [pl.BlockSpec((tm, GROUP_WIDTH), lambda i: (i, 0))] * 4 + [_full(w_out.shape)],
        out_specs=pl.BlockSpec((tm, D_MODEL), lambda i: (i, 0)),
        out_shape=jax.ShapeDtypeStruct((t, D_MODEL), F32),
        compiler_params=_params("arbitrary"), name="outproj",
    )(x2d, *outs, w_out)


def _ffn_body(x_ref, halo_ref, g_ref, win_ref, cw_ref, cb_ref, wout_ref, y_ref, act_ref):
    tm = TM_PROJ
    first = (pl.program_id(0) % (SEQ // tm)) == 0
    g = g_ref[...]

    def norm(v):
        ms = jnp.mean(v * v, axis=-1, keepdims=True)
        return v * lax.rsqrt(ms + EPS) * g

    x = x_ref[...]
    h_halo = jnp.where(first, 0.0, norm(halo_ref[...])).astype(BF16)
    h_ext = jnp.concatenate([h_halo, norm(x).astype(BF16)], axis=0)

    def conv(u, lo, width):
        w = cw_ref[:, lo:lo + width]
        out = cb_ref[:, lo:lo + width] + w[2:3, :] * u[FFN_HALO:, :]
        out = out + w[1:2, :] * pltpu.roll(u, 1, 0)[FFN_HALO:, :]
        return out + w[0:1, :] * pltpu.roll(u, 2, 0)[FFN_HALO:, :]

    for lo, width in FFN_CHUNKS:
        gate = conv(jnp.dot(h_ext, win_ref[:, lo:lo + width], preferred_element_type=F32), lo, width)
        up = conv(jnp.dot(h_ext, win_ref[:, D_FF + lo:D_FF + lo + width], preferred_element_type=F32),
                  D_FF + lo, width)
        act_ref[:, lo:lo + width] = (gate * _sigmoid(gate) * up).astype(BF16)
    y_ref[...] = x + jnp.dot(act_ref[...], wout_ref[...], preferred_element_type=F32)


def _ffn(x2d, prm):
    t = x2d.shape[0]
    tm = TM_PROJ
    per = tm // FFN_HALO
    consts = [prm["ffn_g"], prm["ffn_win"], prm["ffn_cw"], prm["ffn_cb"], prm["ffn_wout"]]
    return pl.pallas_call(
        _ffn_body, grid=(t // tm,),
        in_specs=[pl.BlockSpec((tm, D_MODEL), lambda i: (i, 0)),
                  pl.BlockSpec((FFN_HALO, D_MODEL), lambda i: (jnp.maximum(i * per - 1, 0), 0))]
                 + [_full(a.shape) for a in consts],
        out_specs=pl.BlockSpec((tm, D_MODEL), lambda i: (i, 0)),
        out_shape=jax.ShapeDtypeStruct((t, D_MODEL), F32),
        scratch_shapes=[pltpu.VMEM((tm, D_FF), BF16)],
        compiler_params=_params("arbitrary"), name="ffn",
    )(x2d, x2d, *consts)


def _rope_tables(positions):
    pos = positions.reshape(-1).astype(F32)[:, None]
    inv_a = 1.0 / (ROPE_THETA ** (jnp.arange(0, HEAD_DIM, 2, dtype=F32) / HEAD_DIM))
    ang_a = pos * inv_a
    cosa = jnp.tile(jnp.cos(ang_a), (1, 4))
    sina = jnp.tile(jnp.concatenate([-jnp.sin(ang_a), jnp.sin(ang_a)], axis=1), (1, 2))
    inv_m = 1.0 / (ROPE_THETA ** (jnp.arange(0, MLA_ROPE, 2, dtype=F32) / MLA_ROPE))
    ang_m = pos * inv_m
    n = pos.shape[0]
    ones = jnp.ones((n, MLA_NOPE), F32)
    zeros32 = jnp.zeros((n, LANES - MLA_QK), F32)
    cosm = jnp.concatenate([ones, jnp.cos(ang_m), jnp.cos(ang_m), 1.0 + zeros32], axis=1)
    sinm = jnp.concatenate([0.0 * ones, -jnp.sin(ang_m), jnp.sin(ang_m), zeros32], axis=1)
    return cosa, sina, cosm, sinm


def _pad_heads(w, per_head, width):
    lead = w.shape[:-1]
    w = w.reshape(*lead, N_HEADS, per_head)
    w = jnp.pad(w, [(0, 0)] * len(lead) + [(0, 0), (0, width - per_head)])
    return w.reshape(*lead, N_HEADS * width)


def _layer_params(l, mix_norm_g, w_in, moba_qk_g, mla_q_norm_g, mla_kv_norm_g, mla_w_uq, mla_w_ukv,
                  mla_qk_g, ssm_conv_w, ssm_conv_b, ssm_dt_bias, ssm_a_log, ssm_d, ssm_norm_g,
                  head_out_g, w_out, ffn_norm_g, ffn_w_in, ffn_conv_w, ffn_conv_b, ffn_w_out):
    w = w_in[l]
    o = 0
    moba_w = w[:, o:o + 768]; o += 768
    cq_w = w[:, o:o + MLA_Q_RANK]; o += MLA_Q_RANK
    ckv_w = w[:, o:o + MLA_KV_RANK]; o += MLA_KV_RANK
    kr_w = w[:, o:o + MLA_ROPE]; o += MLA_ROPE
    sb_w = w[:, o:o + 768]; o += 768
    z_w = w[:, o:o + 256]; o += 256
    xbc_w = w[:, o:o + SSM_XBC]; o += SSM_XBC
    dt_w = w[:, o:o + SSM_HEADS]
    zc = lambda n: jnp.zeros((D_MODEL, n), F32)
    half = HEAD_DIM // 2
    swap_a = (jnp.arange(GROUP_WIDTH) // HEAD_DIM) * HEAD_DIM + (jnp.arange(GROUP_WIDTH) % HEAD_DIM + half) % HEAD_DIM
    rh = MLA_ROPE // 2
    swap_r = jnp.concatenate([jnp.arange(rh, MLA_ROPE), jnp.arange(0, rh)])
    in_head = jnp.concatenate([jnp.arange(MLA_NOPE), MLA_NOPE + swap_r, jnp.arange(MLA_QK, LANES)])
    swap_m = (jnp.arange(N_HEADS)[:, None] * LANES + in_head[None, :]).reshape(-1)
    w_r = jnp.concatenate([
        moba_w[:, :512], moba_w[:, :256][:, swap_a], moba_w[:, 256:512][:, swap_a],
        cq_w, zc(256 - MLA_Q_RANK), ckv_w,
        zc(KR_LANE), kr_w, dt_w, zc(LANES - DT_LANE - SSM_HEADS),
        zc(KR_LANE), kr_w[:, swap_r], zc(LANES - KR_LANE - MLA_ROPE),
        sb_w[:, :512], z_w, xbc_w], axis=1).astype(BF16)
    w_vt = jnp.stack([moba_w[:, 512:].T, sb_w[:, 512:].T]).astype(BF16)

    lane_row = lambda v, start: jnp.zeros((1, LANES), F32).at[0, start:start + v.shape[0]].set(v)
    moba_scale = jnp.array([[HEAD_DIM ** -0.5 * LOG2E], [1.0]], F32)
    mla_scale = jnp.array([[MLA_QK ** -0.5 * LOG2E], [1.0]], F32)
    gmoba = jnp.tile(moba_qk_g[l], (1, N_HEADS)) * moba_scale
    gmla = jnp.tile(jnp.pad(mla_qk_g[l], ((0, 0), (0, LANES - MLA_QK))), (1, N_HEADS)) * mla_scale
    wuq = jnp.pad(_pad_heads(mla_w_uq[l], MLA_QK, LANES), ((0, 256 - MLA_Q_RANK), (0, 0)))
    ukv = mla_w_ukv[l].reshape(MLA_KV_RANK, N_HEADS, MLA_NOPE + HEAD_DIM)
    hog = head_out_g[l].reshape(3, GROUP_WIDTH, 1)
    return {
        "mix_g": mix_norm_g[l][None, :], "w_in": w_r, "w_vt": w_vt,
        "gmoba": jnp.concatenate([gmoba, gmoba[:, swap_a]], axis=0),
        "gcq": jnp.pad(mla_q_norm_g[l], (0, 256 - MLA_Q_RANK))[None, :],
        "gckv": mla_kv_norm_g[l][None, :],
        "wuq": jnp.stack([wuq, wuq[:, swap_m]]).astype(BF16),
        "wuk": _pad_heads(ukv[:, :, :MLA_NOPE].reshape(MLA_KV_RANK, -1), MLA_NOPE, LANES).astype(BF16),
        "wuvt": ukv[:, :, MLA_NOPE:].reshape(MLA_KV_RANK, GROUP_WIDTH).T.astype(BF16),
        "gmla": jnp.concatenate([gmla, gmla[:, swap_m]], axis=0),
        "ssm_cw": ssm_conv_w[l], "ssm_cb": ssm_conv_b[l][None, :],
        "ssm_dtb": lane_row(ssm_dt_bias[l], DT_LANE), "ssm_alog": lane_row(ssm_a_log[l], DT_LANE),
        "ssm_d": jnp.repeat(ssm_d[l], HEAD_DIM)[None, :], "ssm_ng": ssm_norm_g[l][None, :],
        "g_moba_out": hog[0], "g_mla_out": hog[1], "g_sb_out": hog[2],
        "w_out": w_out[l].astype(BF16),
        "ffn_g": ffn_norm_g[l][None, :], "ffn_win": ffn_w_in[l].astype(BF16),
        "ffn_cw": ffn_conv_w[l], "ffn_cb": ffn_conv_b[l][None, :], "ffn_wout": ffn_w_out[l].astype(BF16),
    }


def _layer(x2d, bsz, prm):
    (mq, mk, mvt, lq, lk, lvt, sq, sk, svt, z, xbc, dt) = _inproj(x2d, prm)
    r3 = lambda a: a.reshape(bsz, SEQ, a.shape[-1])
    o_moba = _moba(r3(mq), r3(mk), mvt, prm["g_moba_out"])
    o_mla = _mla(r3(lq), r3(lk), lvt, prm["g_mla_out"])
    o_sb = _sb(r3(sq), r3(sk), svt, prm["g_sb_out"])
    o_ssm = _ssd(r3(z), r3(xbc), r3(dt), prm)
    flat = lambda a: a.reshape(bsz * SEQ, GROUP_WIDTH)
    x2d = _outproj(x2d, [flat(o_moba), flat(o_mla), flat(o_sb), flat(o_ssm)], prm["w_out"])
    return _ffn(x2d, prm)


def kernel(x, positions, mix_norm_g, w_in, moba_qk_g, mla_q_norm_g, mla_kv_norm_g, mla_w_uq, mla_w_ukv,
           mla_qk_g, ssm_conv_w, ssm_conv_b, ssm_dt_bias, ssm_a_log, ssm_d, ssm_norm_g, head_out_g, w_out,
           ffn_norm_g, ffn_w_in, ffn_conv_w, ffn_conv_b, ffn_w_out):
    bsz, seq, d = x.shape
    assert seq == SEQ and d == D_MODEL
    cosa, sina, cosm, sinm = _rope_tables(positions)
    x2d = x.reshape(bsz * seq, d)
    for l in range(w_in.shape[0]):
        prm = _layer_params(l, mix_norm_g, w_in, moba_qk_g, mla_q_norm_g, mla_kv_norm_g, mla_w_uq,
                            mla_w_ukv, mla_qk_g, ssm_conv_w, ssm_conv_b, ssm_dt_bias, ssm_a_log, ssm_d,
                            ssm_norm_g, head_out_g, w_out, ffn_norm_g, ffn_w_in, ffn_conv_w, ffn_conv_b,
                            ffn_w_out)
        prm.update(cosa=cosa, sina=sina, cosm=cosm, sinm=sinm)
        x2d = _layer(x2d, bsz, prm)
    return x2d.reshape(bsz, seq, d)
```

```python
import functools
import math

import jax
import jax.numpy as jnp
from jax import lax
from jax.experimental import pallas as pl
from jax.experimental.pallas import tpu as pltpu

F32 = jnp.float32
BF16 = jnp.bfloat16

D_MODEL = 1024
SEQ = 2048
GROUP_WIDTH = 256
HEAD_DIM = 64
N_HEADS = 4
MOBA_BLOCK = 256
MOBA_TOPK = 3
MLA_Q_RANK = 192
MLA_KV_RANK = 128
MLA_NOPE = 64
MLA_ROPE = 32
MLA_QK = MLA_NOPE + MLA_ROPE
SSM_HEADS = 4
SSM_STATE = 64
SSM_CONV = 4
SSM_CHUNK = 128
SSM_XBC = 512
D_FF = 2816
FFN_CONV = 3
ROPE_THETA = 10000.0
EPS = 1e-6

LANES = 128
VMEM_LIMIT_BYTES = 56 * 1024 * 1024
MASK_VALUE = -1e30
LOG2E = math.log2(math.e)

C_MOBA = 0
C_CQ = 1024
C_CKV = 1280
C_KR = 1408
C_KRS = 1536
C_SB = 1664
C_Z = 2176
C_XBC = 2432
KR_LANE = 64
DT_LANE = 96

TM_PROJ = 512
TQ = 256
FFN_HALO = 16
FFN_CHUNKS = ((0, 512), (512, 512), (1024, 512), (1536, 512), (2048, 512), (2560, 256))


def _params(*sem):
    return pltpu.CompilerParams(dimension_semantics=sem, vmem_limit_bytes=VMEM_LIMIT_BYTES)


def _full(shape):
    n = len(shape)
    return pl.BlockSpec(shape, lambda *_: (0,) * n)


def _sigmoid(x):
    return 1.0 / (1.0 + jnp.exp(-x))


def _softplus_neg_abs(x):
    return jnp.log(1.0 + jnp.exp(-jnp.abs(x)))


def _group_sumsq(x, group):
    xx = x * x
    outs = []
    for c in range(x.shape[1] // LANES):
        slab = xx[:, c * LANES:(c + 1) * LANES]
        if group == LANES:
            s = jnp.sum(slab, axis=1, keepdims=True)
            outs.append(jnp.broadcast_to(s, slab.shape))
        else:
            lane = lax.broadcasted_iota(jnp.int32, slab.shape, 1)
            lo = lane < group
            s_lo = jnp.sum(jnp.where(lo, slab, 0.0), axis=1, keepdims=True)
            s_hi = jnp.sum(jnp.where(lo, 0.0, slab), axis=1, keepdims=True)
            outs.append(jnp.where(lo, s_lo, s_hi))
    return outs[0] if len(outs) == 1 else jnp.concatenate(outs, axis=1)


def _nt_dot(a, b):
    return lax.dot_general(a, b, (((1,), (1,)), ((), ())), preferred_element_type=F32)


def _inproj_body(x_ref, g_ref, w_ref, wvt_ref, cosa_ref, sina_ref, cosm_ref, sinm_ref, gmoba_ref, gcq_ref,
                 gckv_ref, wuq_ref, wuk_ref, wuvt_ref, gmla_ref,
                 mq_ref, mk_ref, mvt_ref, lq_ref, lk_ref, lvt_ref, sq_ref, sk_ref, svt_ref,
                 z_ref, xbc_ref, dt_ref):
    x = x_ref[...]
    ms = jnp.mean(x * x, axis=-1, keepdims=True)
    h = (x * lax.rsqrt(ms + EPS) * g_ref[...]).astype(BF16)

    def proj(lo, width):
        return jnp.dot(h, w_ref[:, lo:lo + width], preferred_element_type=F32)

    mvt_ref[...] = _nt_dot(wvt_ref[0], h).astype(BF16)
    svt_ref[...] = _nt_dot(wvt_ref[1], h).astype(BF16)

    def norm_rope(t, t_sw, group, n, gain_ref, idx, cos, sin):
        r = lax.rsqrt(_group_sumsq(t, group) * (1.0 / n) + EPS)
        return (r * ((t * gain_ref[idx:idx + 1, :]) * cos + (t_sw * gain_ref[idx + 2:idx + 3, :]) * sin)).astype(BF16)

    cosa = jnp.concatenate([cosa_ref[...]] * 2, axis=1)
    sina = jnp.concatenate([sina_ref[...]] * 2, axis=1)
    for idx, dst in ((0, mq_ref), (1, mk_ref)):
        t = proj(C_MOBA + idx * GROUP_WIDTH, GROUP_WIDTH)
        t_sw = proj(C_MOBA + (idx + 2) * GROUP_WIDTH, GROUP_WIDTH)
        dst[...] = norm_rope(t, t_sw, HEAD_DIM, HEAD_DIM, gmoba_ref, idx, cosa, sina)

    cq = proj(C_CQ, 256)
    cq = (cq * lax.rsqrt(jnp.sum(cq * cq, axis=-1, keepdims=True) * (1.0 / MLA_Q_RANK) + EPS)
          * gcq_ref[...]).astype(BF16)
    ckv = proj(C_CKV, LANES)
    ckv = (ckv * lax.rsqrt(jnp.mean(ckv * ckv, axis=-1, keepdims=True) + EPS) * gckv_ref[...]).astype(BF16)
    krdt = proj(C_KR, LANES)
    dt_ref[...] = krdt
    lane = lax.broadcasted_iota(jnp.int32, krdt.shape, 1)
    kr = jnp.where((lane >= KR_LANE) & (lane < KR_LANE + MLA_ROPE), krdt, 0.0)
    cosm = jnp.concatenate([cosm_ref[...]] * N_HEADS, axis=1)
    sinm = jnp.concatenate([sinm_ref[...]] * N_HEADS, axis=1)
    ql = jnp.dot(cq, wuq_ref[0], preferred_element_type=F32)
    ql_sw = jnp.dot(cq, wuq_ref[1], preferred_element_type=F32)
    kl = jnp.dot(ckv, wuk_ref[...], preferred_element_type=F32) + jnp.concatenate([kr] * N_HEADS, axis=1)
    kl_sw = jnp.concatenate([proj(C_KRS, LANES)] * N_HEADS, axis=1)
    lq_ref[...] = norm_rope(ql, ql_sw, LANES, MLA_QK, gmla_ref, 0, cosm, sinm)
    lk_ref[...] = norm_rope(kl, kl_sw, LANES, MLA_QK, gmla_ref, 1, cosm, sinm)
    lvt_ref[...] = _nt_dot(wuvt_ref[...], ckv).astype(BF16)

    sq_ref[...] = (proj(C_SB, GROUP_WIDTH) * (HEAD_DIM ** -0.5)).astype(BF16)
    sk_ref[...] = proj(C_SB + GROUP_WIDTH, GROUP_WIDTH).astype(BF16)

    z_ref[...] = proj(C_Z, GROUP_WIDTH)
    xbc_ref[...] = proj(C_XBC, SSM_XBC)


def _inproj(x2d, prm):
    t = x2d.shape[0]
    tm = TM_PROJ
    row = lambda w: pl.BlockSpec((tm, w), lambda i: (i, 0))
    tok = lambda w, dt: (jax.ShapeDtypeStruct((t, w), dt), row(w))
    tr = (jax.ShapeDtypeStruct((GROUP_WIDTH, t), BF16), pl.BlockSpec((GROUP_WIDTH, tm), lambda i: (0, i)))
    outs = [tok(256, BF16), tok(256, BF16), tr, tok(512, BF16), tok(512, BF16), tr,
            tok(256, BF16), tok(256, BF16), tr, tok(256, F32), tok(512, F32), tok(128, F32)]
    out_shapes = [o[0] for o in outs]
    out_specs = [o[1] for o in outs]
    consts = [prm["mix_g"], prm["w_in"], prm["w_vt"]]
    tables = [prm["cosa"], prm["sina"], prm["cosm"], prm["sinm"]]
    tail = [prm["gmoba"], prm["gcq"], prm["gckv"], prm["wuq"], prm["wuk"], prm["wuvt"], prm["gmla"]]
    in_specs = ([row(D_MODEL)] + [_full(a.shape) for a in consts] + [row(LANES)] * 4
                + [_full(a.shape) for a in tail])
    return pl.pallas_call(
        _inproj_body, grid=(t // tm,), in_specs=in_specs, out_specs=out_specs, out_shape=out_shapes,
        compiler_params=_params("arbitrary"), name="inproj",
    )(x2d, *consts, *tables, *tail)


QW = N_HEADS * TQ


def _head_slices(a):
    return [a[:, hd * TQ:(hd + 1) * TQ] for hd in range(N_HEADS)]


def _stack_heads(q):
    head_of_lane = lax.broadcasted_iota(jnp.int32, q.shape, 1) >> 6
    return jnp.concatenate([jnp.where(head_of_lane == hd, q, jnp.zeros_like(q)) for hd in range(N_HEADS)],
                           axis=0)


ONES_ROWS = 16


def _pv(vt_ref, k0, p, with_sum):
    ps = _head_slices(p.astype(BF16))
    outs = []
    for hd in range(N_HEADS):
        lhs = vt_ref[hd * HEAD_DIM:(hd + 1) * HEAD_DIM, pl.ds(k0, TQ)]
        if with_sum:
            r = lax.broadcasted_iota(jnp.int32, (ONES_ROWS, TQ), 0)
            lhs = jnp.concatenate([lhs, jnp.where(r == 0, 1.0, 0.0).astype(BF16)], axis=0)
        outs.append(jnp.dot(lhs, ps[hd], preferred_element_type=F32))
    return outs


def _key_query_iotas():
    key = lax.broadcasted_iota(jnp.int32, (TQ, QW), 0)
    qry = lax.broadcasted_iota(jnp.int32, (TQ, QW), 1) & (TQ - 1)
    return key, qry


def _softmax_init():
    return (jnp.full((1, QW), MASK_VALUE, F32), [jnp.zeros((HEAD_DIM + ONES_ROWS, TQ), F32)] * N_HEADS)


def _softmax_step(s, vt_ref, k0, m_run, acc, keep=None):
    m_new = jnp.maximum(m_run, jnp.max(s, axis=0, keepdims=True))
    m_sub = m_new
    if keep is not None:
        m_new = jnp.where(keep, m_new, m_run)
        m_sub = jnp.where(keep, m_new, -MASK_VALUE)
    al = _head_slices(jnp.exp2(m_run - m_new))
    pv = _pv(vt_ref, k0, jnp.exp2(s - m_sub), True)
    return m_new, [a * al[hd] + x for hd, (a, x) in enumerate(zip(acc, pv))]


ATTN_ROWS = 2


class _Chain:
    def __init__(self, s_a, s_b, scores, step, last=None):
        self.s_a, self.s_b, self.scores, self.step, self.last = s_a, s_b, scores, step, last


def _pipelined(i, chains, block_of, carries):
    def fetch(c, n):
        return c.scores(jnp.clip(block_of(n), 0, i))

    for c in chains:
        c.s_a[...] = fetch(c, 0)

    def pair(t, carries):
        n0 = 2 * t
        for c in chains:
            c.s_b[...] = fetch(c, n0 + 1)
        carries = [c.step(c.s_a[...], block_of(n0), cr) for c, cr in zip(chains, carries)]
        for c in chains:
            c.s_a[...] = fetch(c, n0 + 2)
        return [c.step(c.s_b[...], block_of(n0 + 1), cr) for c, cr in zip(chains, carries)]

    carries = lax.fori_loop(0, lax.shift_right_logical(i, 1), pair, carries)

    def tail_even(carries):
        return [cr if c.last is None else c.last(c.s_a[...], cr) for c, cr in zip(chains, carries)]

    def tail_odd(carries):
        for c in chains:
            if c.last is not None:
                c.s_b[...] = fetch(c, i)
        carries = [c.step(c.s_a[...], block_of(i - 1), cr) for c, cr in zip(chains, carries)]
        return [cr if c.last is None else c.last(c.s_b[...], cr) for c, cr in zip(chains, carries)]

    return lax.cond((i & 1) == 1, tail_odd, tail_even, carries)


def _write_heads(o_ref, row, acc, gain_ref, normalise):
    outs = []
    for hd in range(N_HEADS):
        o = acc[hd][:HEAD_DIM]
        if normalise:
            o = o * (1.0 / acc[hd][HEAD_DIM:HEAD_DIM + 1])
        ms = jnp.mean(o * o, axis=0, keepdims=True)
        outs.append(o * lax.rsqrt(ms + EPS) * gain_ref[hd * HEAD_DIM:(hd + 1) * HEAD_DIM, :])
    o_ref[row] = jnp.concatenate(outs, axis=0).T.astype(BF16)


def _key_col(row, blk):
    return pl.multiple_of(row * SEQ + blk * TQ, TQ)


def _moba_body(q_ref, k_ref, vt_ref, gout_ref, o_ref, sa_ref, sb_ref, kmean_ref, sel_ref):
    i = pl.program_id(1)
    nblk = SEQ // MOBA_BLOCK

    @pl.when(i == 0)
    def _():
        kmean_ref[...] = jnp.zeros(kmean_ref.shape, F32)
        for row in range(ATTN_ROWS):
            for n in range(nblk):
                kb = k_ref[row, n * MOBA_BLOCK:(n + 1) * MOBA_BLOCK, :].astype(F32)
                kmean_ref[row, n:n + 1, :] = jnp.mean(kb, axis=0, keepdims=True)

    key, qry = _key_query_iotas()

    def chain(row):
        qs = _stack_heads(q_ref[row])

        km = kmean_ref[row]
        km_hi = km.astype(BF16)
        km_lo = (km - km_hi.astype(F32)).astype(BF16)
        gate = _nt_dot(km_hi, qs) + _nt_dot(km_lo, qs)
        blk = lax.broadcasted_iota(jnp.int32, gate.shape, 0)
        rank = jnp.zeros(gate.shape, F32)
        for m in range(nblk - 1):
            gm = gate[m:m + 1, :]
            beats = (gm > gate) | ((gm == gate) & (blk > m))
            rank = rank + jnp.where(beats & (i > m), 1.0, 0.0)
        sel_ref[row] = jnp.where((blk < i) & (rank < float(MOBA_TOPK)), 1.0, 0.0)

        def scores(blk):
            return _nt_dot(k_ref[row, pl.ds(pl.multiple_of(blk * TQ, TQ), TQ), :], qs)

        def step(s, blk, carry):
            return _softmax_step(s, vt_ref, _key_col(row, blk), *carry, keep=sel_ref[row, pl.ds(blk, 1), :] > 0.5)

        def diagonal(s, carry):
            return _softmax_step(jnp.where(key <= qry, s, MASK_VALUE), vt_ref, _key_col(row, i), *carry)

        return _Chain(sa_ref.at[row], sb_ref.at[row], scores, step, diagonal)

    chains = [chain(row) for row in range(ATTN_ROWS)]
    carries = _pipelined(i, chains, lambda n: n, [_softmax_init()] * ATTN_ROWS)
    for row, (_, acc) in enumerate(carries):
        _write_heads(o_ref, row, acc, gout_ref, True)


def _attn_call(body, name, q, k, vt, extra, scratch):
    b, _, qw = q.shape
    kw = k.shape[-1]
    rows = ATTN_ROWS
    assert b % rows == 0
    return pl.pallas_call(
        body, grid=(b // rows, SEQ // TQ),
        in_specs=[pl.BlockSpec((rows, TQ, qw), lambda g, i: (g, i, 0)),
                  pl.BlockSpec((rows, SEQ, kw), lambda g, i: (g, 0, 0)),
                  pl.BlockSpec((GROUP_WIDTH, rows * SEQ), lambda g, i: (0, g))]
                 + [_full(a.shape) for a in extra],
        out_specs=pl.BlockSpec((rows, TQ, GROUP_WIDTH), lambda g, i: (g, i, 0)),
        out_shape=jax.ShapeDtypeStruct((b, SEQ, GROUP_WIDTH), BF16),
        scratch_shapes=[pltpu.VMEM((rows, TQ, QW), F32), pltpu.VMEM((rows, TQ, QW), F32)] + scratch,
        compiler_params=_params("arbitrary", "arbitrary"), name=name,
    )(q, k, vt, *extra)


def _moba(q, k, vt, gout):
    scratch = [pltpu.VMEM((ATTN_ROWS, 16, 256), F32), pltpu.VMEM((ATTN_ROWS, 16, QW), F32)]
    return _attn_call(_moba_body, "moba", q, k, vt, [gout], scratch)


def _mla_body(q_ref, k_ref, vt_ref, gout_ref, o_ref, sa_ref, sb_ref):
    i = pl.program_id(1)
    key, qry = _key_query_iotas()

    def chain(row):
        qh = [q_ref[row, :, hd * LANES:(hd + 1) * LANES] for hd in range(N_HEADS)]

        def scores(blk):
            k0 = pl.multiple_of(blk * TQ, TQ)
            return jnp.concatenate(
                [_nt_dot(k_ref[row, pl.ds(k0, TQ), hd * LANES:(hd + 1) * LANES], qh[hd])
                 for hd in range(N_HEADS)], axis=1)

        def step(s, blk, carry):
            return _softmax_step(s, vt_ref, _key_col(row, blk), *carry)

        def diagonal(s, carry):
            return _softmax_step(jnp.where(key <= qry, s, MASK_VALUE), vt_ref, _key_col(row, i), *carry)

        return _Chain(sa_ref.at[row], sb_ref.at[row], scores, step, diagonal)

    chains = [chain(row) for row in range(ATTN_ROWS)]
    carries = _pipelined(i, chains, lambda n: n, [_softmax_init()] * ATTN_ROWS)
    for row, (_, acc) in enumerate(carries):
        _write_heads(o_ref, row, acc, gout_ref, True)


def _mla(q, k, vt, gout):
    return _attn_call(_mla_body, "mla", q, k, vt, [gout], [])


def _sb_body(q_ref, k_ref, vt_ref, uu_ref, gout_ref, o_ref, sa_ref, sb_ref):
    i = pl.program_id(1)
    key, qry = _key_query_iotas()
    strict = key < qry

    def neg_log_survive(z):
        return jnp.maximum(z, 0.0) + jnp.log(1.0 + jnp.exp2(jnp.abs(z) * (-LOG2E)))

    def suffix_sums(c):
        hi = c.astype(BF16)
        lo = (c - hi.astype(F32)).astype(BF16)
        t = jnp.dot(uu_ref[...], jnp.concatenate([hi, lo], axis=0), preferred_element_type=F32)
        return t[:TQ], t[TQ:TQ + 1]

    def chain(row):
        qs = _stack_heads(q_ref[row])

        def scores(blk):
            return _nt_dot(k_ref[row, pl.ds(pl.multiple_of(blk * TQ, TQ), TQ), :], qs)

        def step(z, blk, carry, diagonal=False):
            csum, acc = carry
            c = neg_log_survive(z)
            if diagonal:
                c = jnp.where(strict, c, 0.0)
            incl, total = suffix_sums(c)
            a = jnp.exp(z - (incl + csum))
            if diagonal:
                a = jnp.where(strict, a, 0.0)
            pv = _pv(vt_ref, _key_col(row, blk), a, False)
            return csum + total, [x + y for x, y in zip(acc, pv)]

        return _Chain(sa_ref.at[row], sb_ref.at[row], scores, step)

    chains = [chain(row) for row in range(ATTN_ROWS)]
    zero = (jnp.zeros((1, QW), F32), [jnp.zeros((HEAD_DIM, TQ), F32)] * N_HEADS)
    carries = [c.step(c.scores(i), i, zero, diagonal=True) for c in chains]
    carries = _pipelined(i, chains, lambda n: i - 1 - n, carries)
    for row, (_, acc) in enumerate(carries):
        _write_heads(o_ref, row, acc, gout_ref, False)


SB_SUM_ROWS = TQ + 16


def _sb(q, k, vt, gout):
    r = lax.broadcasted_iota(jnp.int32, (SB_SUM_ROWS, 2 * TQ), 0)
    c = lax.broadcasted_iota(jnp.int32, (SB_SUM_ROWS, 2 * TQ), 1) & (TQ - 1)
    uu = jnp.where((c >= r) | (r == TQ), 1.0, 0.0).astype(BF16)
    return _attn_call(_sb_body, "stickbreak", q, k, vt, [uu, gout], [])


def _ssd_body(z_ref, xbc_ref, dt_ref, cw_ref, cb_ref, dtb_ref, alog_ref, dskip_ref, ng_ref, o_ref,
              halo_ref, st_ref):
    c = pl.program_id(1)
    cl = SSM_CHUNK

    @pl.when(c == 0)
    def _():
        halo_ref[...] = jnp.zeros(halo_ref.shape, F32)
        st_ref[...] = jnp.zeros(st_ref.shape, F32)

    for row in range(SSD_ROWS):
        _ssd_chunk(row, z_ref, xbc_ref, dt_ref, cw_ref, cb_ref, dtb_ref, alog_ref, dskip_ref, ng_ref,
                   o_ref, halo_ref, st_ref)


def _ssd_chunk(row, z_ref, xbc_ref, dt_ref, cw_ref, cb_ref, dtb_ref, alog_ref, dskip_ref, ng_ref, o_ref,
               halo_ref, st_ref):
    cl = SSM_CHUNK
    xb = xbc_ref[row]
    ext = jnp.concatenate([halo_ref[row], xb], axis=0)
    u = cb_ref[...] + cw_ref[SSM_CONV - 1:SSM_CONV, :] * xb
    for tap in range(SSM_CONV - 1):
        shift = SSM_CONV - 1 - tap
        u = u + cw_ref[tap:tap + 1, :] * pltpu.roll(ext, shift, 0)[8:8 + cl]
    halo_ref[row] = xb[cl - 8:cl]
    xc = u * _sigmoid(u)
    xs = xc[:, 0:GROUP_WIDTH]
    bm = xc[:, GROUP_WIDTH:GROUP_WIDTH + LANES]
    cm = xc[:, GROUP_WIDTH + LANES:SSM_XBC]

    lane = lax.broadcasted_iota(jnp.int32, (cl, LANES), 1)
    head_lane = (lane >= DT_LANE) & (lane < DT_LANE + SSM_HEADS)
    dtv = dt_ref[row] + dtb_ref[...]
    dt = jnp.where(head_lane, jnp.maximum(dtv, 0.0) + _softplus_neg_abs(dtv), 0.0)
    a = dt * (-jnp.exp(alog_ref[...]))

    r = lax.broadcasted_iota(jnp.int32, (cl, cl), 0)
    s = lax.broadcasted_iota(jnp.int32, (cl, cl), 1)
    tril = s <= r
    tril_b = jnp.where(tril, 1.0, 0.0).astype(BF16)
    a1 = a.astype(BF16)
    a2 = (a - a1.astype(F32)).astype(BF16)
    a3 = (a - a1.astype(F32) - a2.astype(F32)).astype(BF16)
    acum = (jnp.dot(tril_b, a1, preferred_element_type=F32)
            + (jnp.dot(tril_b, a2, preferred_element_type=F32)
               + jnp.dot(tril_b, a3, preferred_element_type=F32)))
    acum_t = acum.T

    lane2 = lax.broadcasted_iota(jnp.int32, (cl, GROUP_WIDTH), 1)
    head2 = lane2 >> 6

    def per_head(v):
        out = jnp.broadcast_to(v[:, DT_LANE + 3:DT_LANE + 4], (cl, GROUP_WIDTH))
        for hd in (2, 1, 0):
            out = jnp.where(head2 == hd, v[:, DT_LANE + hd:DT_LANE + hd + 1], out)
        return out

    dt_full = per_head(dt)
    acum_full = per_head(acum)
    last_full = acum_full[cl - 1:cl, :]
    xdt = xs * dt_full
    xdt_b = xdt.astype(BF16)

    grp = lane >> 6
    bm_b = bm.astype(BF16)
    y = jnp.zeros((cl, GROUP_WIDTH), F32)
    for g in range(2):
        cg = jnp.where(grp == g, cm, 0.0).astype(BF16)
        gmat = _nt_dot(cg, bm_b)
        for hd in (2 * g, 2 * g + 1):
            seg = acum[:, DT_LANE + hd:DT_LANE + hd + 1] - acum_t[DT_LANE + hd:DT_LANE + hd + 1, :]
            lmat = jnp.where(tril, jnp.exp(jnp.where(tril, seg, 0.0)), 0.0)
            yd = jnp.dot((gmat * lmat).astype(BF16), xdt_b, preferred_element_type=F32)
            y = jnp.where(head2 == hd, yd, y)

    st_prev = st_ref[row]
    y_off = jnp.exp(acum_full) * jnp.dot(cm.astype(BF16), st_prev.astype(BF16), preferred_element_type=F32)
    dec = jnp.exp(last_full - acum_full)
    s_new = jnp.dot(bm.T.astype(BF16), (xdt * dec).astype(BF16), preferred_element_type=F32)
    rs = lax.broadcasted_iota(jnp.int32, (LANES, GROUP_WIDTH), 0) >> 6
    cs = lax.broadcasted_iota(jnp.int32, (LANES, GROUP_WIDTH), 1) >> 7
    st_ref[row] = jnp.exp(last_full) * st_prev + jnp.where(rs == cs, s_new, 0.0)

    y = y + y_off + dskip_ref[...] * xs
    zz = z_ref[row]
    y = y * (zz * _sigmoid(zz))
    ms = jnp.mean(y * y, axis=-1, keepdims=True)
    o_ref[row] = (y * lax.rsqrt(ms + EPS) * ng_ref[...]).astype(BF16)


SSD_ROWS = 4


def _ssd(z, xbc, dt, prm):
    b = z.shape[0]
    cl = SSM_CHUNK
    rows = SSD_ROWS
    assert b % rows == 0
    small = [prm["ssm_cw"], prm["ssm_cb"], prm["ssm_dtb"], prm["ssm_alog"], prm["ssm_d"], prm["ssm_ng"]]
    return pl.pallas_call(
        _ssd_body, grid=(b // rows, SEQ // cl),
        in_specs=[pl.BlockSpec((rows, cl, 256), lambda g, c: (g, c, 0)),
                  pl.BlockSpec((rows, cl, 512), lambda g, c: (g, c, 0)),
                  pl.BlockSpec((rows, cl, LANES), lambda g, c: (g, c, 0))]
                 + [_full(a.shape) for a in small],
        out_specs=pl.BlockSpec((rows, cl, 256), lambda g, c: (g, c, 0)),
        out_shape=jax.ShapeDtypeStruct((b, SEQ, 256), BF16),
        scratch_shapes=[pltpu.VMEM((rows, 8, SSM_XBC), F32), pltpu.VMEM((rows, LANES, GROUP_WIDTH), F32)],
        compiler_params=_params("arbitrary", "arbitrary"), name="ssd",
    )(z, xbc, dt, *small)


def _outproj_body(x_ref, o1_ref, o2_ref, o3_ref, o4_ref, w_ref, y_ref):
    acc = x_ref[...]
    for n, o_ref in enumerate((o1_ref, o2_ref, o3_ref, o4_ref)):
        acc = acc + jnp.dot(o_ref[...], w_ref[n * GROUP_WIDTH:(n + 1) * GROUP_WIDTH, :],
                            preferred_element_type=F32)
    y_ref[...] = acc


def _outproj(x2d, outs, w_out):
    t = x2d.shape[0]
    tm = TM_PROJ
    return pl.pallas_call(
        _outproj_body, grid=(t // tm,),
        in_specs=[pl.BlockSpec((tm, D_MODEL), lambda i: (i, 0))]
                 + [pl.BlockSpec((tm, GROUP_WIDTH), lambda i: (i, 0))] * 4 + [_full(w_out.shape)],
        out_specs=pl.BlockSpec((tm, D_MODEL), lambda i: (i, 0)),
        out_shape=jax.ShapeDtypeStruct((t, D_MODEL), F32),
        compiler_params=_params("arbitrary"), name="outproj",
    )(x2d, *outs, w_out)


def _ffn_body(x_ref, halo_ref, g_ref, win_ref, cw_ref, cb_ref, wout_ref, y_ref, act_ref):
    tm = TM_PROJ
    first = (pl.program_id(0) % (SEQ // tm)) == 0
    g = g_ref[...]

    def norm(v):
        ms = jnp.mean(v * v, axis=-1, keepdims=True)
        return v * lax.rsqrt(ms + EPS) * g

    x = x_ref[...]
    h_halo = jnp.where(first, 0.0, norm(halo_ref[...])).astype(BF16)
    h_ext = jnp.concatenate([h_halo, norm(x).astype(BF16)], axis=0)

    def conv(u, lo, width):
        w = cw_ref[:, lo:lo + width]
        out = cb_ref[:, lo:lo + width] + w[2:3, :] * u[FFN_HALO:, :]
        out = out + w[1:2, :] * pltpu.roll(u, 1, 0)[FFN_HALO:, :]
        return out + w[0:1, :] * pltpu.roll(u, 2, 0)[FFN_HALO:, :]

    for lo, width in FFN_CHUNKS:
        gate = conv(jnp.dot(h_ext, win_ref[:, lo:lo + width], preferred_element_type=F32), lo, width)
        up = conv(jnp.dot(h_ext, win_ref[:, D_FF + lo:D_FF + lo + width], preferred_element_type=F32),
                  D_FF + lo, width)
        act_ref[:, lo:lo + width] = (gate * _sigmoid(gate) * up).astype(BF16)
    y_ref[...] = x + jnp.dot(act_ref[...], wout_ref[...], preferred_element_type=F32)


def _ffn(x2d, prm):
    t = x2d.shape[0]
    tm = TM_PROJ
    per = tm // FFN_HALO
    consts = [prm["ffn_g"], prm["ffn_win"], prm["ffn_cw"], prm["ffn_cb"], prm["ffn_wout"]]
    return pl.pallas_call(
        _ffn_body, grid=(t // tm,),
        in_specs=[pl.BlockSpec((tm, D_MODEL), lambda i: (i, 0)),
                  pl.BlockSpec((FFN_HALO, D_MODEL), lambda i: (jnp.maximum(i * per - 1, 0), 0))]
                 + [_full(a.shape) for a in consts],
        out_specs=pl.BlockSpec((tm, D_MODEL), lambda i: (i, 0)),
        out_shape=jax.ShapeDtypeStruct((t, D_MODEL), F32),
        scratch_shapes=[pltpu.VMEM((tm, D_FF), BF16)],
        compiler_params=_params("arbitrary"), name="ffn",
    )(x2d, x2d, *consts)


def _rope_tables(positions):
    pos = positions.reshape(-1).astype(F32)[:, None]
    inv_a = 1.0 / (ROPE_THETA ** (jnp.arange(0, HEAD_DIM, 2, dtype=F32) / HEAD_DIM))
    inv_m = 1.0 / (ROPE_THETA ** (jnp.arange(0, MLA_ROPE, 2, dtype=F32) / MLA_ROPE))
    ang = pos * jnp.concatenate([inv_a, inv_m])
    cos_all, sin_all = lax.optimization_barrier((jnp.cos(ang), jnp.sin(ang)))
    na = HEAD_DIM // 2
    cos_a, sin_a, cos_m, sin_m = cos_all[:, :na], sin_all[:, :na], cos_all[:, na:], sin_all[:, na:]
    cosa = jnp.tile(cos_a, (1, 4))
    sina = jnp.tile(jnp.concatenate([-sin_a, sin_a], axis=1), (1, 2))
    n = pos.shape[0]
    ones = jnp.ones((n, MLA_NOPE), F32)
    zeros32 = jnp.zeros((n, LANES - MLA_QK), F32)
    cosm = jnp.concatenate([ones, cos_m, cos_m, 1.0 + zeros32], axis=1)
    sinm = jnp.concatenate([0.0 * ones, -sin_m, sin_m, zeros32], axis=1)
    return cosa, sina, cosm, sinm


def _pad_heads(w, per_head, width):
    lead = w.shape[:-1]
    w = w.reshape(*lead, N_HEADS, per_head)
    w = jnp.pad(w, [(0, 0)] * len(lead) + [(0, 0), (0, width - per_head)])
    return w.reshape(*lead, N_HEADS * width)


def _layer_params(l, mix_norm_g, w_in, moba_qk_g, mla_q_norm_g, mla_kv_norm_g, mla_w_uq, mla_w_ukv,
                  mla_qk_g, ssm_conv_w, ssm_conv_b, ssm_dt_bias, ssm_a_log, ssm_d, ssm_norm_g,
                  head_out_g, w_out, ffn_norm_g, ffn_w_in, ffn_conv_w, ffn_conv_b, ffn_w_out):
    w = w_in[l].astype(BF16)
    o = 0
    moba_w = w[:, o:o + 768]; o += 768
    cq_w = w[:, o:o + MLA_Q_RANK]; o += MLA_Q_RANK
    ckv_w = w[:, o:o + MLA_KV_RANK]; o += MLA_KV_RANK
    kr_w = w[:, o:o + MLA_ROPE]; o += MLA_ROPE
    sb_w = w[:, o:o + 768]; o += 768
    z_w = w[:, o:o + 256]; o += 256
    xbc_w = w[:, o:o + SSM_XBC]; o += SSM_XBC
    dt_w = w[:, o:o + SSM_HEADS]
    zc = lambda n: jnp.zeros((D_MODEL, n), BF16)
    half = HEAD_DIM // 2
    swap_a = (jnp.arange(GROUP_WIDTH) // HEAD_DIM) * HEAD_DIM + (jnp.arange(GROUP_WIDTH) % HEAD_DIM + half) % HEAD_DIM
    rh = MLA_ROPE // 2
    swap_r = jnp.concatenate([jnp.arange(rh, MLA_ROPE), jnp.arange(0, rh)])
    in_head = jnp.concatenate([jnp.arange(MLA_NOPE), MLA_NOPE + swap_r, jnp.arange(MLA_QK, LANES)])
    swap_m = (jnp.arange(N_HEADS)[:, None] * LANES + in_head[None, :]).reshape(-1)
    w_r = jnp.concatenate([
        moba_w[:, :512], moba_w[:, :256][:, swap_a], moba_w[:, 256:512][:, swap_a],
        cq_w, zc(256 - MLA_Q_RANK), ckv_w,
        zc(KR_LANE), kr_w, dt_w, zc(LANES - DT_LANE - SSM_HEADS),
        zc(KR_LANE), kr_w[:, swap_r], zc(LANES - KR_LANE - MLA_ROPE),
        sb_w[:, :512], z_w, xbc_w], axis=1)
    w_vt = jnp.stack([moba_w[:, 512:].T, sb_w[:, 512:].T])

    lane_row = lambda v, start: jnp.zeros((1, LANES), F32).at[0, start:start + v.shape[0]].set(v)
    moba_scale = jnp.array([[HEAD_DIM ** -0.5 * LOG2E], [1.0]], F32)
    mla_scale = jnp.array([[MLA_QK ** -0.5 * LOG2E], [1.0]], F32)
    gmoba = jnp.tile(moba_qk_g[l], (1, N_HEADS)) * moba_scale
    gmla = jnp.tile(jnp.pad(mla_qk_g[l], ((0, 0), (0, LANES - MLA_QK))), (1, N_HEADS)) * mla_scale
    wuq = jnp.pad(_pad_heads(mla_w_uq[l], MLA_QK, LANES), ((0, 256 - MLA_Q_RANK), (0, 0)))
    ukv = mla_w_ukv[l].reshape(MLA_KV_RANK, N_HEADS, MLA_NOPE + HEAD_DIM)
    hog = head_out_g[l].reshape(3, GROUP_WIDTH, 1)
    return {
        "mix_g": mix_norm_g[l][None, :], "w_in": w_r, "w_vt": w_vt,
        "gmoba": jnp.concatenate([gmoba, gmoba[:, swap_a]], axis=0),
        "gcq": jnp.pad(mla_q_norm_g[l], (0, 256 - MLA_Q_RANK))[None, :],
        "gckv": mla_kv_norm_g[l][None, :],
        "wuq": jnp.stack([wuq, wuq[:, swap_m]]).astype(BF16),
        "wuk": _pad_heads(ukv[:, :, :MLA_NOPE].reshape(MLA_KV_RANK, -1), MLA_NOPE, LANES).astype(BF16),
        "wuvt": ukv[:, :, MLA_NOPE:].reshape(MLA_KV_RANK, GROUP_WIDTH).T.astype(BF16),
        "gmla": jnp.concatenate([gmla, gmla[:, swap_m]], axis=0),
        "ssm_cw": ssm_conv_w[l], "ssm_cb": ssm_conv_b[l][None, :],
        "ssm_dtb": lane_row(ssm_dt_bias[l], DT_LANE), "ssm_alog": lane_row(ssm_a_log[l], DT_LANE),
        "ssm_d": jnp.repeat(ssm_d[l], HEAD_DIM)[None, :], "ssm_ng": ssm_norm_g[l][None, :],
        "g_moba_out": hog[0], "g_mla_out": hog[1], "g_sb_out": hog[2],
        "w_out": w_out[l].astype(BF16),
        "ffn_g": ffn_norm_g[l][None, :], "ffn_win": ffn_w_in[l].astype(BF16),
        "ffn_cw": ffn_conv_w[l], "ffn_cb": ffn_conv_b[l][None, :], "ffn_wout": ffn_w_out[l].astype(BF16),
    }


def _layer(x2d, bsz, prm):
    (mq, mk, mvt, lq, lk, lvt, sq, sk, svt, z, xbc, dt) = _inproj(x2d, prm)
    r3 = lambda a: a.reshape(bsz, SEQ, a.shape[-1])
    o_moba = _moba(r3(mq), r3(mk), mvt, prm["g_moba_out"])
    o_mla = _mla(r3(lq), r3(lk), lvt, prm["g_mla_out"])
    o_sb = _sb(r3(sq), r3(sk), svt, prm["g_sb_out"])
    o_ssm = _ssd(r3(z), r3(xbc), r3(dt), prm)
    flat = lambda a: a.reshape(bsz * SEQ, GROUP_WIDTH)
    x2d = _outproj(x2d, [flat(o_moba), flat(o_mla), flat(o_sb), flat(o_ssm)], prm["w_out"])
    return _ffn(x2d, prm)


def kernel(x, positions, mix_norm_g, w_in, moba_qk_g, mla_q_norm_g, mla_kv_norm_g, mla_w_uq, mla_w_ukv,
           mla_qk_g, ssm_conv_w, ssm_conv_b, ssm_dt_bias, ssm_a_log, ssm_d, ssm_norm_g, head_out_g, w_out,
           ffn_norm_g, ffn_w_in, ffn_conv_w, ffn_conv_b, ffn_w_out):
    bsz, seq, d = x.shape
    assert seq == SEQ and d == D_MODEL
    cosa, sina, cosm, sinm = _rope_tables(positions)
    x2d = x.reshape(bsz * seq, d)
    for l in range(w_in.shape[0]):
        prm = _layer_params(l, mix_norm_g, w_in, moba_qk_g, mla_q_norm_g, mla_kv_norm_g, mla_w_uq,
                            mla_w_ukv, mla_qk_g, ssm_conv_w, ssm_conv_b, ssm_dt_bias, ssm_a_log, ssm_d,
                            ssm_norm_g, head_out_g, w_out, ffn_norm_g, ffn_w_in, ffn_conv_w, ffn_conv_b,
                            ffn_w_out)
        prm.update(cosa=cosa, sina=sina, cosm=cosm, sinm=sinm)
        x2d = _layer(x2d, bsz, prm)
    return x2d.reshape(bsz, seq, d)
```

```python
import functools
import math

import numpy as np
import jax
import jax.numpy as jnp
from jax import lax
from jax.experimental import pallas as pl
from jax.experimental.pallas import tpu as pltpu

F32 = jnp.float32
BF16 = jnp.bfloat16

D_MODEL = 1024
SEQ = 2048
GROUP_WIDTH = 256
HEAD_DIM = 64
N_HEADS = 4
MOBA_BLOCK = 256
MOBA_TOPK = 3
MLA_Q_RANK = 192
MLA_KV_RANK = 128
MLA_NOPE = 64
MLA_ROPE = 32
MLA_QK = MLA_NOPE + MLA_ROPE
SSM_HEADS = 4
SSM_STATE = 64
SSM_CONV = 4
SSM_CHUNK = 128
SSM_XBC = 512
D_FF = 2816
FFN_CONV = 3
ROPE_THETA = 10000.0
EPS = 1e-6

LANES = 128
VMEM_LIMIT_BYTES = 56 * 1024 * 1024
MASK_VALUE = -1e30
LOG2E = math.log2(math.e)

C_MOBA = 0
C_CQ = 1024
C_CKV = 1280
C_KR = 1408
C_KRS = 1536
C_SB = 1664
C_Z = 2176
C_XBC = 2432
KR_LANE = 64
DT_LANE = 96

TM_PROJ = 512
TQ = 256
FFN_HALO = 16
FFN_CHUNKS = ((0, 512), (512, 512), (1024, 512), (1536, 512), (2048, 512), (2560, 256))


def _params(*sem):
    return pltpu.CompilerParams(dimension_semantics=sem, vmem_limit_bytes=VMEM_LIMIT_BYTES)


def _full(shape):
    n = len(shape)
    return pl.BlockSpec(shape, lambda *_: (0,) * n)


def _sigmoid(x):
    return 1.0 / (1.0 + jnp.exp(-x))


def _softplus_neg_abs(x):
    return jnp.log(1.0 + jnp.exp(-jnp.abs(x)))


def _group_sumsq(x, group):
    xx = x * x
    outs = []
    for c in range(x.shape[1] // LANES):
        slab = xx[:, c * LANES:(c + 1) * LANES]
        if group == LANES:
            s = jnp.sum(slab, axis=1, keepdims=True)
            outs.append(jnp.broadcast_to(s, slab.shape))
        else:
            lane = lax.broadcasted_iota(jnp.int32, slab.shape, 1)
            lo = lane < group
            s_lo = jnp.sum(jnp.where(lo, slab, 0.0), axis=1, keepdims=True)
            s_hi = jnp.sum(jnp.where(lo, 0.0, slab), axis=1, keepdims=True)
            outs.append(jnp.where(lo, s_lo, s_hi))
    return outs[0] if len(outs) == 1 else jnp.concatenate(outs, axis=1)


def _nt_dot(a, b):
    return lax.dot_general(a, b, (((1,), (1,)), ((), ())), preferred_element_type=F32)


def _inproj_body(x_ref, g_ref, w_ref, wvt_ref, cosa_ref, sina_ref, cosm_ref, sinm_ref, gmoba_ref, gcq_ref,
                 gckv_ref, wuq_ref, wuk_ref, wuvt_ref, gmla_ref,
                 mq_ref, mk_ref, mvt_ref, lq_ref, lk_ref, lvt_ref, sq_ref, sk_ref, svt_ref,
                 z_ref, xbc_ref, dt_ref):
    x = x_ref[...]
    ms = jnp.mean(x * x, axis=-1, keepdims=True)
    h = (x * lax.rsqrt(ms + EPS) * g_ref[...]).astype(BF16)

    def proj(lo, width):
        return jnp.dot(h, w_ref[:, lo:lo + width], preferred_element_type=F32)

    mvt_ref[...] = _nt_dot(wvt_ref[0], h).astype(BF16)
    svt_ref[...] = _nt_dot(wvt_ref[1], h).astype(BF16)

    def norm_rope(t, t_sw, group, n, gain_ref, idx, cos, sin):
        r = lax.rsqrt(_group_sumsq(t, group) * (1.0 / n) + EPS)
        return (r * ((t * gain_ref[idx:idx + 1, :]) * cos + (t_sw * gain_ref[idx + 2:idx + 3, :]) * sin)).astype(BF16)

    cosa = jnp.concatenate([cosa_ref[...]] * 2, axis=1)
    sina = jnp.concatenate([sina_ref[...]] * 2, axis=1)
    for idx, dst in ((0, mq_ref), (1, mk_ref)):
        t = proj(C_MOBA + idx * GROUP_WIDTH, GROUP_WIDTH)
        t_sw = proj(C_MOBA + (idx + 2) * GROUP_WIDTH, GROUP_WIDTH)
        dst[...] = norm_rope(t, t_sw, HEAD_DIM, HEAD_DIM, gmoba_ref, idx, cosa, sina)

    cq = proj(C_CQ, 256)
    cq = (cq * lax.rsqrt(jnp.sum(cq * cq, axis=-1, keepdims=True) * (1.0 / MLA_Q_RANK) + EPS)
          * gcq_ref[...]).astype(BF16)
    ckv = proj(C_CKV, LANES)
    ckv = (ckv * lax.rsqrt(jnp.mean(ckv * ckv, axis=-1, keepdims=True) + EPS) * gckv_ref[...]).astype(BF16)
    krdt = proj(C_KR, LANES)
    dt_ref[...] = krdt
    lane = lax.broadcasted_iota(jnp.int32, krdt.shape, 1)
    kr = jnp.where((lane >= KR_LANE) & (lane < KR_LANE + MLA_ROPE), krdt, 0.0)
    cosm = jnp.concatenate([cosm_ref[...]] * N_HEADS, axis=1)
    sinm = jnp.concatenate([sinm_ref[...]] * N_HEADS, axis=1)
    ql = jnp.dot(cq, wuq_ref[0], preferred_element_type=F32)
    ql_sw = jnp.dot(cq, wuq_ref[1], preferred_element_type=F32)
    kl = jnp.dot(ckv, wuk_ref[...], preferred_element_type=F32) + jnp.concatenate([kr] * N_HEADS, axis=1)
    kl_sw = jnp.concatenate([proj(C_KRS, LANES)] * N_HEADS, axis=1)
    lq_ref[...] = norm_rope(ql, ql_sw, LANES, MLA_QK, gmla_ref, 0, cosm, sinm)
    lk_ref[...] = norm_rope(kl, kl_sw, LANES, MLA_QK, gmla_ref, 1, cosm, sinm)
    lvt_ref[...] = _nt_dot(wuvt_ref[...], ckv).astype(BF16)

    sq_ref[...] = (proj(C_SB, GROUP_WIDTH) * (HEAD_DIM ** -0.5)).astype(BF16)
    sk_ref[...] = proj(C_SB + GROUP_WIDTH, GROUP_WIDTH).astype(BF16)

    z_ref[...] = proj(C_Z, GROUP_WIDTH)
    xbc_ref[...] = proj(C_XBC, SSM_XBC)


def _inproj(x2d, prm):
    t = x2d.shape[0]
    tm = TM_PROJ
    row = lambda w: pl.BlockSpec((tm, w), lambda i: (i, 0))
    tok = lambda w, dt: (jax.ShapeDtypeStruct((t, w), dt), row(w))
    tr = (jax.ShapeDtypeStruct((GROUP_WIDTH, t), BF16), pl.BlockSpec((GROUP_WIDTH, tm), lambda i: (0, i)))
    outs = [tok(256, BF16), tok(256, BF16), tr, tok(512, BF16), tok(512, BF16), tr,
            tok(256, BF16), tok(256, BF16), tr, tok(256, F32), tok(512, F32), tok(128, F32)]
    out_shapes = [o[0] for o in outs]
    out_specs = [o[1] for o in outs]
    consts = [prm["mix_g"], prm["w_in"], prm["w_vt"]]
    tables = [prm["rope"]] * 4
    table_specs = [pl.BlockSpec((tm, LANES), functools.partial(lambda j, i: (i, j), j)) for j in range(4)]
    tail = [prm["gmoba"], prm["gcq"], prm["gckv"], prm["wuq"], prm["wuk"], prm["wuvt"], prm["gmla"]]
    in_specs = ([row(D_MODEL)] + [_full(a.shape) for a in consts] + table_specs
                + [_full(a.shape) for a in tail])
    return pl.pallas_call(
        _inproj_body, grid=(t // tm,), in_specs=in_specs, out_specs=out_specs, out_shape=out_shapes,
        compiler_params=_params("arbitrary"), name="inproj",
    )(x2d, *consts, *tables, *tail)


QW = N_HEADS * TQ


def _head_slices(a):
    return [a[:, hd * TQ:(hd + 1) * TQ] for hd in range(N_HEADS)]


def _stack_heads(q):
    head_of_lane = lax.broadcasted_iota(jnp.int32, q.shape, 1) >> 6
    return jnp.concatenate([jnp.where(head_of_lane == hd, q, jnp.zeros_like(q)) for hd in range(N_HEADS)],
                           axis=0)


ONES_ROWS = 16


def _pv(vt_ref, k0, p, with_sum):
    ps = _head_slices(p.astype(BF16))
    outs = []
    for hd in range(N_HEADS):
        lhs = vt_ref[hd * HEAD_DIM:(hd + 1) * HEAD_DIM, pl.ds(k0, TQ)]
        if with_sum:
            r = lax.broadcasted_iota(jnp.int32, (ONES_ROWS, TQ), 0)
            lhs = jnp.concatenate([lhs, jnp.where(r == 0, 1.0, 0.0).astype(BF16)], axis=0)
        outs.append(jnp.dot(lhs, ps[hd], preferred_element_type=F32))
    return outs


def _key_query_iotas():
    key = lax.broadcasted_iota(jnp.int32, (TQ, QW), 0)
    qry = lax.broadcasted_iota(jnp.int32, (TQ, QW), 1) & (TQ - 1)
    return key, qry


def _softmax_init():
    return (jnp.full((1, QW), MASK_VALUE, F32), [jnp.zeros((HEAD_DIM + ONES_ROWS, TQ), F32)] * N_HEADS)


def _softmax_step(s, vt_ref, k0, m_run, acc, keep=None):
    m_new = jnp.maximum(m_run, jnp.max(s, axis=0, keepdims=True))
    m_sub = m_new
    if keep is not None:
        m_new = jnp.where(keep, m_new, m_run)
        m_sub = jnp.where(keep, m_new, -MASK_VALUE)
    al = _head_slices(jnp.exp2(m_run - m_new))
    pv = _pv(vt_ref, k0, jnp.exp2(s - m_sub), True)
    return m_new, [a * al[hd] + x for hd, (a, x) in enumerate(zip(acc, pv))]


ATTN_ROWS = 2


class _Chain:
    def __init__(self, s_a, s_b, scores, step, last=None):
        self.s_a, self.s_b, self.scores, self.step, self.last = s_a, s_b, scores, step, last


def _pipelined(i, chains, block_of, carries):
    def fetch(c, n):
        return c.scores(jnp.clip(block_of(n), 0, i))

    for c in chains:
        c.s_a[...] = fetch(c, 0)

    def pair(t, carries):
        n0 = 2 * t
        for c in chains:
            c.s_b[...] = fetch(c, n0 + 1)
        carries = [c.step(c.s_a[...], block_of(n0), cr) for c, cr in zip(chains, carries)]
        for c in chains:
            c.s_a[...] = fetch(c, n0 + 2)
        return [c.step(c.s_b[...], block_of(n0 + 1), cr) for c, cr in zip(chains, carries)]

    carries = lax.fori_loop(0, lax.shift_right_logical(i, 1), pair, carries)

    def tail_even(carries):
        return [cr if c.last is None else c.last(c.s_a[...], cr) for c, cr in zip(chains, carries)]

    def tail_odd(carries):
        for c in chains:
            if c.last is not None:
                c.s_b[...] = fetch(c, i)
        carries = [c.step(c.s_a[...], block_of(i - 1), cr) for c, cr in zip(chains, carries)]
        return [cr if c.last is None else c.last(c.s_b[...], cr) for c, cr in zip(chains, carries)]

    return lax.cond((i & 1) == 1, tail_odd, tail_even, carries)


def _write_heads(o_ref, row, acc, gain_ref, normalise):
    outs = []
    for hd in range(N_HEADS):
        o = acc[hd][:HEAD_DIM]
        if normalise:
            o = o * (1.0 / acc[hd][HEAD_DIM:HEAD_DIM + 1])
        ms = jnp.mean(o * o, axis=0, keepdims=True)
        outs.append(o * lax.rsqrt(ms + EPS) * gain_ref[hd * HEAD_DIM:(hd + 1) * HEAD_DIM, :])
    o_ref[row] = jnp.concatenate(outs, axis=0).T.astype(BF16)


def _key_col(row, blk):
    return pl.multiple_of(row * SEQ + blk * TQ, TQ)


def _moba_body(q_ref, k_ref, vt_ref, gout_ref, o_ref, sa_ref, sb_ref, kmean_ref, sel_ref):
    i = pl.program_id(1)
    nblk = SEQ // MOBA_BLOCK

    @pl.when(i == 0)
    def _():
        kmean_ref[...] = jnp.zeros(kmean_ref.shape, F32)
        for row in range(ATTN_ROWS):
            for n in range(nblk):
                kb = k_ref[row, n * MOBA_BLOCK:(n + 1) * MOBA_BLOCK, :].astype(F32)
                kmean_ref[row, n:n + 1, :] = jnp.mean(kb, axis=0, keepdims=True)

    key, qry = _key_query_iotas()

    def chain(row):
        qs = _stack_heads(q_ref[row])

        km = kmean_ref[row]
        km_hi = km.astype(BF16)
        km_lo = (km - km_hi.astype(F32)).astype(BF16)
        gate = _nt_dot(km_hi, qs) + _nt_dot(km_lo, qs)
        blk = lax.broadcasted_iota(jnp.int32, gate.shape, 0)
        rank = jnp.zeros(gate.shape, F32)
        for m in range(nblk - 1):
            gm = gate[m:m + 1, :]
            beats = (gm > gate) | ((gm == gate) & (blk > m))
            rank = rank + jnp.where(beats & (i > m), 1.0, 0.0)
        sel_ref[row] = jnp.where((blk < i) & (rank < float(MOBA_TOPK)), 1.0, 0.0)

        def scores(blk):
            return _nt_dot(k_ref[row, pl.ds(pl.multiple_of(blk * TQ, TQ), TQ), :], qs)

        def step(s, blk, carry):
            return _softmax_step(s, vt_ref, _key_col(row, blk), *carry, keep=sel_ref[row, pl.ds(blk, 1), :] > 0.5)

        def diagonal(s, carry):
            return _softmax_step(jnp.where(key <= qry, s, MASK_VALUE), vt_ref, _key_col(row, i), *carry)

        return _Chain(sa_ref.at[row], sb_ref.at[row], scores, step, diagonal)

    chains = [chain(row) for row in range(ATTN_ROWS)]
    carries = _pipelined(i, chains, lambda n: n, [_softmax_init()] * ATTN_ROWS)
    for row, (_, acc) in enumerate(carries):
        _write_heads(o_ref, row, acc, gout_ref, True)


def _attn_call(body, name, q, k, vt, extra, scratch):
    b, _, qw = q.shape
    kw = k.shape[-1]
    rows = ATTN_ROWS
    assert b % rows == 0
    return pl.pallas_call(
        body, grid=(b // rows, SEQ // TQ),
        in_specs=[pl.BlockSpec((rows, TQ, qw), lambda g, i: (g, i, 0)),
                  pl.BlockSpec((rows, SEQ, kw), lambda g, i: (g, 0, 0)),
                  pl.BlockSpec((GROUP_WIDTH, rows * SEQ), lambda g, i: (0, g))]
                 + [_full(a.shape) for a in extra],
        out_specs=pl.BlockSpec((rows, TQ, GROUP_WIDTH), lambda g, i: (g, i, 0)),
        out_shape=jax.ShapeDtypeStruct((b, SEQ, GROUP_WIDTH), BF16),
        scratch_shapes=[pltpu.VMEM((rows, TQ, QW), F32), pltpu.VMEM((rows, TQ, QW), F32)] + scratch,
        compiler_params=_params("arbitrary", "arbitrary"), name=name,
    )(q, k, vt, *extra)


def _moba(q, k, vt, gout):
    scratch = [pltpu.VMEM((ATTN_ROWS, 16, 256), F32), pltpu.VMEM((ATTN_ROWS, 16, QW), F32)]
    return _attn_call(_moba_body, "moba", q, k, vt, [gout], scratch)


def _mla_body(q_ref, k_ref, vt_ref, gout_ref, o_ref, sa_ref, sb_ref):
    i = pl.program_id(1)
    key, qry = _key_query_iotas()

    def chain(row):
        qh = [q_ref[row, :, hd * LANES:(hd + 1) * LANES] for hd in range(N_HEADS)]

        def scores(blk):
            k0 = pl.multiple_of(blk * TQ, TQ)
            return jnp.concatenate(
                [_nt_dot(k_ref[row, pl.ds(k0, TQ), hd * LANES:(hd + 1) * LANES], qh[hd])
                 for hd in range(N_HEADS)], axis=1)

        def step(s, blk, carry):
            return _softmax_step(s, vt_ref, _key_col(row, blk), *carry)

        def diagonal(s, carry):
            return _softmax_step(jnp.where(key <= qry, s, MASK_VALUE), vt_ref, _key_col(row, i), *carry)

        return _Chain(sa_ref.at[row], sb_ref.at[row], scores, step, diagonal)

    chains = [chain(row) for row in range(ATTN_ROWS)]
    carries = _pipelined(i, chains, lambda n: n, [_softmax_init()] * ATTN_ROWS)
    for row, (_, acc) in enumerate(carries):
        _write_heads(o_ref, row, acc, gout_ref, True)


def _mla(q, k, vt, gout):
    return _attn_call(_mla_body, "mla", q, k, vt, [gout], [])


def _sb_body(q_ref, k_ref, vt_ref, uu_ref, gout_ref, o_ref, sa_ref, sb_ref):
    i = pl.program_id(1)
    key, qry = _key_query_iotas()
    strict = key < qry

    def neg_log_survive(z):
        return jnp.maximum(z, 0.0) + jnp.log(1.0 + jnp.exp2(jnp.abs(z) * (-LOG2E)))

    def suffix_sums(c):
        hi = c.astype(BF16)
        lo = (c - hi.astype(F32)).astype(BF16)
        t = jnp.dot(uu_ref[...], jnp.concatenate([hi, lo], axis=0), preferred_element_type=F32)
        return t[:TQ], t[TQ:TQ + 1]

    def chain(row):
        qs = _stack_heads(q_ref[row])

        def scores(blk):
            return _nt_dot(k_ref[row, pl.ds(pl.multiple_of(blk * TQ, TQ), TQ), :], qs)

        def step(z, blk, carry, diagonal=False):
            csum, acc = carry
            c = neg_log_survive(z)
            if diagonal:
                c = jnp.where(strict, c, 0.0)
            incl, total = suffix_sums(c)
            a = jnp.exp(z - (incl + csum))
            if diagonal:
                a = jnp.where(strict, a, 0.0)
            pv = _pv(vt_ref, _key_col(row, blk), a, False)
            return csum + total, [x + y for x, y in zip(acc, pv)]

        return _Chain(sa_ref.at[row], sb_ref.at[row], scores, step)

    chains = [chain(row) for row in range(ATTN_ROWS)]
    zero = (jnp.zeros((1, QW), F32), [jnp.zeros((HEAD_DIM, TQ), F32)] * N_HEADS)
    carries = [c.step(c.scores(i), i, zero, diagonal=True) for c in chains]
    carries = _pipelined(i, chains, lambda n: i - 1 - n, carries)
    for row, (_, acc) in enumerate(carries):
        _write_heads(o_ref, row, acc, gout_ref, False)


SB_SUM_ROWS = TQ + 16


def _sb(q, k, vt, gout):
    r = lax.broadcasted_iota(jnp.int32, (SB_SUM_ROWS, 2 * TQ), 0)
    c = lax.broadcasted_iota(jnp.int32, (SB_SUM_ROWS, 2 * TQ), 1) & (TQ - 1)
    uu = jnp.where((c >= r) | (r == TQ), 1.0, 0.0).astype(BF16)
    return _attn_call(_sb_body, "stickbreak", q, k, vt, [uu, gout], [])


def _ssd_body(z_ref, xbc_ref, dt_ref, cw_ref, cb_ref, dtb_ref, alog_ref, dskip_ref, ng_ref, o_ref,
              halo_ref, st_ref):
    c = pl.program_id(1)
    cl = SSM_CHUNK

    @pl.when(c == 0)
    def _():
        halo_ref[...] = jnp.zeros(halo_ref.shape, F32)
        st_ref[...] = jnp.zeros(st_ref.shape, F32)

    for row in range(SSD_ROWS):
        _ssd_chunk(row, z_ref, xbc_ref, dt_ref, cw_ref, cb_ref, dtb_ref, alog_ref, dskip_ref, ng_ref,
                   o_ref, halo_ref, st_ref)


def _ssd_chunk(row, z_ref, xbc_ref, dt_ref, cw_ref, cb_ref, dtb_ref, alog_ref, dskip_ref, ng_ref, o_ref,
               halo_ref, st_ref):
    cl = SSM_CHUNK
    xb = xbc_ref[row]
    ext = jnp.concatenate([halo_ref[row], xb], axis=0)
    u = cb_ref[...] + cw_ref[SSM_CONV - 1:SSM_CONV, :] * xb
    for tap in range(SSM_CONV - 1):
        shift = SSM_CONV - 1 - tap
        u = u + cw_ref[tap:tap + 1, :] * pltpu.roll(ext, shift, 0)[8:8 + cl]
    halo_ref[row] = xb[cl - 8:cl]
    xc = u * _sigmoid(u)
    xs = xc[:, 0:GROUP_WIDTH]
    bm = xc[:, GROUP_WIDTH:GROUP_WIDTH + LANES]
    cm = xc[:, GROUP_WIDTH + LANES:SSM_XBC]

    lane = lax.broadcasted_iota(jnp.int32, (cl, LANES), 1)
    head_lane = (lane >= DT_LANE) & (lane < DT_LANE + SSM_HEADS)
    dtv = dt_ref[row] + dtb_ref[...]
    dt = jnp.where(head_lane, jnp.maximum(dtv, 0.0) + _softplus_neg_abs(dtv), 0.0)
    a = dt * (-jnp.exp(alog_ref[...]))

    r = lax.broadcasted_iota(jnp.int32, (cl, cl), 0)
    s = lax.broadcasted_iota(jnp.int32, (cl, cl), 1)
    tril = s <= r
    tril_b = jnp.where(tril, 1.0, 0.0).astype(BF16)
    a1 = a.astype(BF16)
    a2 = (a - a1.astype(F32)).astype(BF16)
    a3 = (a - a1.astype(F32) - a2.astype(F32)).astype(BF16)
    acum = (jnp.dot(tril_b, a1, preferred_element_type=F32)
            + (jnp.dot(tril_b, a2, preferred_element_type=F32)
               + jnp.dot(tril_b, a3, preferred_element_type=F32)))
    acum_t = acum.T

    lane2 = lax.broadcasted_iota(jnp.int32, (cl, GROUP_WIDTH), 1)
    head2 = lane2 >> 6

    def per_head(v):
        out = jnp.broadcast_to(v[:, DT_LANE + 3:DT_LANE + 4], (cl, GROUP_WIDTH))
        for hd in (2, 1, 0):
            out = jnp.where(head2 == hd, v[:, DT_LANE + hd:DT_LANE + hd + 1], out)
        return out

    dt_full = per_head(dt)
    acum_full = per_head(acum)
    last_full = acum_full[cl - 1:cl, :]
    xdt = xs * dt_full
    xdt_b = xdt.astype(BF16)

    grp = lane >> 6
    bm_b = bm.astype(BF16)
    y = jnp.zeros((cl, GROUP_WIDTH), F32)
    for g in range(2):
        cg = jnp.where(grp == g, cm, 0.0).astype(BF16)
        gmat = _nt_dot(cg, bm_b)
        for hd in (2 * g, 2 * g + 1):
            seg = acum[:, DT_LANE + hd:DT_LANE + hd + 1] - acum_t[DT_LANE + hd:DT_LANE + hd + 1, :]
            lmat = jnp.where(tril, jnp.exp(jnp.where(tril, seg, 0.0)), 0.0)
            yd = jnp.dot((gmat * lmat).astype(BF16), xdt_b, preferred_element_type=F32)
            y = jnp.where(head2 == hd, yd, y)

    st_prev = st_ref[row]
    y_off = jnp.exp(acum_full) * jnp.dot(cm.astype(BF16), st_prev.astype(BF16), preferred_element_type=F32)
    dec = jnp.exp(last_full - acum_full)
    s_new = jnp.dot(bm.T.astype(BF16), (xdt * dec).astype(BF16), preferred_element_type=F32)
    rs = lax.broadcasted_iota(jnp.int32, (LANES, GROUP_WIDTH), 0) >> 6
    cs = lax.broadcasted_iota(jnp.int32, (LANES, GROUP_WIDTH), 1) >> 7
    st_ref[row] = jnp.exp(last_full) * st_prev + jnp.where(rs == cs, s_new, 0.0)

    y = y + y_off + dskip_ref[...] * xs
    zz = z_ref[row]
    y = y * (zz * _sigmoid(zz))
    ms = jnp.mean(y * y, axis=-1, keepdims=True)
    o_ref[row] = (y * lax.rsqrt(ms + EPS) * ng_ref[...]).astype(BF16)


SSD_ROWS = 4


def _ssd(z, xbc, dt, prm):
    b = z.shape[0]
    cl = SSM_CHUNK
    rows = SSD_ROWS
    assert b % rows == 0
    small = [prm["ssm_cw"], prm["ssm_cb"], prm["ssm_dtb"], prm["ssm_alog"], prm["ssm_d"], prm["ssm_ng"]]
    return pl.pallas_call(
        _ssd_body, grid=(b // rows, SEQ // cl),
        in_specs=[pl.BlockSpec((rows, cl, 256), lambda g, c: (g, c, 0)),
                  pl.BlockSpec((rows, cl, 512), lambda g, c: (g, c, 0)),
                  pl.BlockSpec((rows, cl, LANES), lambda g, c: (g, c, 0))]
                 + [_full(a.shape) for a in small],
        out_specs=pl.BlockSpec((rows, cl, 256), lambda g, c: (g, c, 0)),
        out_shape=jax.ShapeDtypeStruct((b, SEQ, 256), BF16),
        scratch_shapes=[pltpu.VMEM((rows, 8, SSM_XBC), F32), pltpu.VMEM((rows, LANES, GROUP_WIDTH), F32)],
        compiler_params=_params("arbitrary", "arbitrary"), name="ssd",
    )(z, xbc, dt, *small)


def _ffn_body(x_ref, o1_ref, o2_ref, o3_ref, o4_ref, xh_ref, oh1_ref, oh2_ref, oh3_ref, oh4_ref,
              wmix_ref, g_ref, win_ref, cw_ref, cb_ref, wout_ref, y_ref, act_ref):
    tm = TM_PROJ
    first = (pl.program_id(0) % (SEQ // tm)) == 0
    g = g_ref[...]

    def mix(xr, o_refs):
        acc = xr[...]
        for n, o_ref in enumerate(o_refs):
            acc = acc + jnp.dot(o_ref[...], wmix_ref[n * GROUP_WIDTH:(n + 1) * GROUP_WIDTH, :],
                                preferred_element_type=F32)
        return acc

    def norm(v):
        ms = jnp.mean(v * v, axis=-1, keepdims=True)
        return v * lax.rsqrt(ms + EPS) * g

    x = mix(x_ref, (o1_ref, o2_ref, o3_ref, o4_ref))
    x_halo = mix(xh_ref, (oh1_ref, oh2_ref, oh3_ref, oh4_ref))
    h_halo = jnp.where(first, 0.0, norm(x_halo)).astype(BF16)
    h_ext = jnp.concatenate([h_halo, norm(x).astype(BF16)], axis=0)

    def conv(u, lo, width):
        w = cw_ref[:, lo:lo + width]
        out = cb_ref[:, lo:lo + width] + w[2:3, :] * u[FFN_HALO:, :]
        out = out + w[1:2, :] * pltpu.roll(u, 1, 0)[FFN_HALO:, :]
        return out + w[0:1, :] * pltpu.roll(u, 2, 0)[FFN_HALO:, :]

    for lo, width in FFN_CHUNKS:
        gate = conv(jnp.dot(h_ext, win_ref[:, lo:lo + width], preferred_element_type=F32), lo, width)
        up = conv(jnp.dot(h_ext, win_ref[:, D_FF + lo:D_FF + lo + width], preferred_element_type=F32),
                  D_FF + lo, width)
        act_ref[:, lo:lo + width] = (gate * _sigmoid(gate) * up).astype(BF16)
    y_ref[...] = x + jnp.dot(act_ref[...], wout_ref[...], preferred_element_type=F32)


def _mix_ffn(x2d, outs, prm):
    t = x2d.shape[0]
    tm = TM_PROJ
    per = tm // FFN_HALO
    consts = [prm["w_out"], prm["ffn_g"], prm["ffn_win"], prm["ffn_cw"], prm["ffn_cb"], prm["ffn_wout"]]
    main = lambda w: pl.BlockSpec((tm, w), lambda i: (i, 0))
    halo = lambda w: pl.BlockSpec((FFN_HALO, w), lambda i: (jnp.maximum(i * per - 1, 0), 0))
    return pl.pallas_call(
        _ffn_body, grid=(t // tm,),
        in_specs=[main(D_MODEL)] + [main(GROUP_WIDTH)] * 4 + [halo(D_MODEL)] + [halo(GROUP_WIDTH)] * 4
                 + [_full(a.shape) for a in consts],
        out_specs=pl.BlockSpec((tm, D_MODEL), lambda i: (i, 0)),
        out_shape=jax.ShapeDtypeStruct((t, D_MODEL), F32),
        scratch_shapes=[pltpu.VMEM((tm, D_FF), BF16)],
        compiler_params=_params("arbitrary"), name="mix_ffn",
    )(x2d, *outs, x2d, *outs, *consts)


def _rope_tables(positions):
    pos = positions.reshape(-1).astype(F32)[:, None]
    inv_a = 1.0 / (ROPE_THETA ** (jnp.arange(0, HEAD_DIM, 2, dtype=F32) / HEAD_DIM))
    inv_m = 1.0 / (ROPE_THETA ** (jnp.arange(0, MLA_ROPE, 2, dtype=F32) / MLA_ROPE))
    ang = pos * jnp.concatenate([inv_a, inv_m])
    cos_all, sin_all = lax.optimization_barrier((jnp.cos(ang), jnp.sin(ang)))
    na, nm = HEAD_DIM // 2, MLA_ROPE // 2
    n_ang = na + nm
    sel = np.zeros((2 * n_ang + 1, 4 * LANES), np.float32)
    for lane in range(LANES):
        sel[lane % na, lane] = 1.0
        sel[n_ang + lane % na, LANES + lane] = -1.0 if lane % HEAD_DIM < na else 1.0
        r = lane - MLA_NOPE
        if 0 <= r < MLA_ROPE:
            sel[na + r % nm, 2 * LANES + lane] = 1.0
            sel[n_ang + na + r % nm, 3 * LANES + lane] = -1.0 if r < nm else 1.0
        else:
            sel[2 * n_ang, 2 * LANES + lane] = 1.0
    feats = jnp.concatenate([cos_all, sin_all, jnp.ones_like(pos)], axis=1)
    return jnp.dot(feats, jnp.asarray(sel), precision=lax.Precision.HIGHEST)


def _pad_heads(w, per_head, width):
    lead = w.shape[:-1]
    w = w.reshape(*lead, N_HEADS, per_head)
    w = jnp.pad(w, [(0, 0)] * len(lead) + [(0, 0), (0, width - per_head)])
    return w.reshape(*lead, N_HEADS * width)


def _layer_params(l, mix_norm_g, w_in, moba_qk_g, mla_q_norm_g, mla_kv_norm_g, mla_w_uq, mla_w_ukv,
                  mla_qk_g, ssm_conv_w, ssm_conv_b, ssm_dt_bias, ssm_a_log, ssm_d, ssm_norm_g,
                  head_out_g, w_out, ffn_norm_g, ffn_w_in, ffn_conv_w, ffn_conv_b, ffn_w_out):
    w = w_in[l].astype(BF16)
    o = 0
    moba_w = w[:, o:o + 768]; o += 768
    cq_w = w[:, o:o + MLA_Q_RANK]; o += MLA_Q_RANK
    ckv_w = w[:, o:o + MLA_KV_RANK]; o += MLA_KV_RANK
    kr_w = w[:, o:o + MLA_ROPE]; o += MLA_ROPE
    sb_w = w[:, o:o + 768]; o += 768
    z_w = w[:, o:o + 256]; o += 256
    xbc_w = w[:, o:o + SSM_XBC]; o += SSM_XBC
    dt_w = w[:, o:o + SSM_HEADS]
    zc = lambda n: jnp.zeros((D_MODEL, n), BF16)
    half = HEAD_DIM // 2
    swap_a = (jnp.arange(GROUP_WIDTH) // HEAD_DIM) * HEAD_DIM + (jnp.arange(GROUP_WIDTH) % HEAD_DIM + half) % HEAD_DIM
    rh = MLA_ROPE // 2
    swap_r = jnp.concatenate([jnp.arange(rh, MLA_ROPE), jnp.arange(0, rh)])
    in_head = jnp.concatenate([jnp.arange(MLA_NOPE), MLA_NOPE + swap_r, jnp.arange(MLA_QK, LANES)])
    swap_m = (jnp.arange(N_HEADS)[:, None] * LANES + in_head[None, :]).reshape(-1)
    w_r = jnp.concatenate([
        moba_w[:, :512], moba_w[:, :256][:, swap_a], moba_w[:, 256:512][:, swap_a],
        cq_w, zc(256 - MLA_Q_RANK), ckv_w,
        zc(KR_LANE), kr_w, dt_w, zc(LANES - DT_LANE - SSM_HEADS),
        zc(KR_LANE), kr_w[:, swap_r], zc(LANES - KR_LANE - MLA_ROPE),
        sb_w[:, :512], z_w, xbc_w], axis=1)
    w_vt = jnp.stack([moba_w[:, 512:].T, sb_w[:, 512:].T])

    lane_row = lambda v, start: jnp.zeros((1, LANES), F32).at[0, start:start + v.shape[0]].set(v)
    moba_scale = jnp.array([[HEAD_DIM ** -0.5 * LOG2E], [1.0]], F32)
    mla_scale = jnp.array([[MLA_QK ** -0.5 * LOG2E], [1.0]], F32)
    gmoba = jnp.tile(moba_qk_g[l], (1, N_HEADS)) * moba_scale
    gmla = jnp.tile(jnp.pad(mla_qk_g[l], ((0, 0), (0, LANES - MLA_QK))), (1, N_HEADS)) * mla_scale
    wuq = jnp.pad(_pad_heads(mla_w_uq[l], MLA_QK, LANES), ((0, 256 - MLA_Q_RANK), (0, 0)))
    ukv = mla_w_ukv[l].reshape(MLA_KV_RANK, N_HEADS, MLA_NOPE + HEAD_DIM)
    hog = head_out_g[l].reshape(3, GROUP_WIDTH, 1)
    return {
        "mix_g": mix_norm_g[l][None, :], "w_in": w_r, "w_vt": w_vt,
        "gmoba": jnp.concatenate([gmoba, gmoba[:, swap_a]], axis=0),
        "gcq": jnp.pad(mla_q_norm_g[l], (0, 256 - MLA_Q_RANK))[None, :],
        "gckv": mla_kv_norm_g[l][None, :],
        "wuq": jnp.stack([wuq, wuq[:, swap_m]]).astype(BF16),
        "wuk": _pad_heads(ukv[:, :, :MLA_NOPE].reshape(MLA_KV_RANK, -1), MLA_NOPE, LANES).astype(BF16),
        "wuvt": ukv[:, :, MLA_NOPE:].reshape(MLA_KV_RANK, GROUP_WIDTH).T.astype(BF16),
        "gmla": jnp.concatenate([gmla, gmla[:, swap_m]], axis=0),
        "ssm_cw": ssm_conv_w[l], "ssm_cb": ssm_conv_b[l][None, :],
        "ssm_dtb": lane_row(ssm_dt_bias[l], DT_LANE), "ssm_alog": lane_row(ssm_a_log[l], DT_LANE),
        "ssm_d": jnp.repeat(ssm_d[l], HEAD_DIM)[None, :], "ssm_ng": ssm_norm_g[l][None, :],
        "g_moba_out": hog[0], "g_mla_out": hog[1], "g_sb_out": hog[2],
        "w_out": w_out[l].astype(BF16),
        "ffn_g": ffn_norm_g[l][None, :], "ffn_win": ffn_w_in[l].astype(BF16),
        "ffn_cw": ffn_conv_w[l], "ffn_cb": ffn_conv_b[l][None, :], "ffn_wout": ffn_w_out[l].astype(BF16),
    }


def _layer(x2d, bsz, prm):
    (mq, mk, mvt, lq, lk, lvt, sq, sk, svt, z, xbc, dt) = _inproj(x2d, prm)
    r3 = lambda a: a.reshape(bsz, SEQ, a.shape[-1])
    o_moba = _moba(r3(mq), r3(mk), mvt, prm["g_moba_out"])
    o_mla = _mla(r3(lq), r3(lk), lvt, prm["g_mla_out"])
    o_sb = _sb(r3(sq), r3(sk), svt, prm["g_sb_out"])
    o_ssm = _ssd(r3(z), r3(xbc), r3(dt), prm)
    flat = lambda a: a.reshape(bsz * SEQ, GROUP_WIDTH)
    return _mix_ffn(x2d, [flat(o_moba), flat(o_mla), flat(o_sb), flat(o_ssm)], prm)


def kernel(x, positions, mix_norm_g, w_in, moba_qk_g, mla_q_norm_g, mla_kv_norm_g, mla_w_uq, mla_w_ukv,
           mla_qk_g, ssm_conv_w, ssm_conv_b, ssm_dt_bias, ssm_a_log, ssm_d, ssm_norm_g, head_out_g, w_out,
           ffn_norm_g, ffn_w_in, ffn_conv_w, ffn_conv_b, ffn_w_out):
    bsz, seq, d = x.shape
    assert seq == SEQ and d == D_MODEL
    rope = _rope_tables(positions)
    x2d = x.reshape(bsz * seq, d)
    for l in range(w_in.shape[0]):
        prm = _layer_params(l, mix_norm_g, w_in, moba_qk_g, mla_q_norm_g, mla_kv_norm_g, mla_w_uq,
                            mla_w_ukv, mla_qk_g, ssm_conv_w, ssm_conv_b, ssm_dt_bias, ssm_a_log, ssm_d,
                            ssm_norm_g, head_out_g, w_out, ffn_norm_g, ffn_w_in, ffn_conv_w, ffn_conv_b,
                            ffn_w_out)
        prm["rope"] = rope
        x2d = _layer(x2d, bsz, prm)
    return x2d.reshape(bsz, seq, d)
```

```python
import functools
import math

import numpy as np
import jax
import jax.numpy as jnp
from jax import lax
from jax.experimental import pallas as pl
from jax.experimental.pallas import tpu as pltpu

F32 = jnp.float32
BF16 = jnp.bfloat16

D_MODEL = 1024
SEQ = 2048
GROUP_WIDTH = 256
HEAD_DIM = 64
N_HEADS = 4
MOBA_BLOCK = 256
MOBA_TOPK = 3
MLA_Q_RANK = 192
MLA_KV_RANK = 128
MLA_NOPE = 64
MLA_ROPE = 32
MLA_QK = MLA_NOPE + MLA_ROPE
SSM_HEADS = 4
SSM_STATE = 64
SSM_CONV = 4
SSM_CHUNK = 128
SSM_XBC = 512
D_FF = 2816
FFN_CONV = 3
ROPE_THETA = 10000.0
EPS = 1e-6

LANES = 128
VMEM_LIMIT_BYTES = 56 * 1024 * 1024
MASK_VALUE = -1e30
LOG2E = math.log2(math.e)

C_MOBA = 0
C_CQ = 1024
C_CKV = 1280
C_KR = 1408
C_SB = 1536
C_Z = 2048
C_XBC = 2304
KR_LANE = 64
DT_LANE = 96

TM_PROJ = 512
TQ = 256
FFN_HALO = 16
FFN_CHUNKS = ((0, 512), (512, 512), (1024, 512), (1536, 512), (2048, 512), (2560, 256))


def _params(*sem):
    return pltpu.CompilerParams(dimension_semantics=sem, vmem_limit_bytes=VMEM_LIMIT_BYTES)


def _full(shape):
    n = len(shape)
    return pl.BlockSpec(shape, lambda *_: (0,) * n)


def _sigmoid(x):
    return 1.0 / (1.0 + jnp.exp(-x))


def _softplus_neg_abs(x):
    return jnp.log(1.0 + jnp.exp(-jnp.abs(x)))


def _group_sumsq(x, group):
    xx = x * x
    outs = []
    for c in range(x.shape[1] // LANES):
        slab = xx[:, c * LANES:(c + 1) * LANES]
        if group == LANES:
            s = jnp.sum(slab, axis=1, keepdims=True)
            outs.append(jnp.broadcast_to(s, slab.shape))
        else:
            lane = lax.broadcasted_iota(jnp.int32, slab.shape, 1)
            lo = lane < group
            s_lo = jnp.sum(jnp.where(lo, slab, 0.0), axis=1, keepdims=True)
            s_hi = jnp.sum(jnp.where(lo, 0.0, slab), axis=1, keepdims=True)
            outs.append(jnp.where(lo, s_lo, s_hi))
    return outs[0] if len(outs) == 1 else jnp.concatenate(outs, axis=1)


def _nt_dot(a, b):
    return lax.dot_general(a, b, (((1,), (1,)), ((), ())), preferred_element_type=F32)


def _inproj_body(x_ref, g_ref, w_ref, wvt_ref, cosa_ref, sina_ref, cosm_ref, sinm_ref, gmoba_ref, gcq_ref,
                 gckv_ref, wuq_ref, wuk_ref, wuvt_ref, gmla_ref,
                 mq_ref, mk_ref, mvt_ref, lq_ref, lk_ref, lvt_ref, sq_ref, sk_ref, svt_ref,
                 z_ref, xbc_ref, dt_ref):
    x = x_ref[...]
    ms = jnp.mean(x * x, axis=-1, keepdims=True)
    h = (x * lax.rsqrt(ms + EPS) * g_ref[...]).astype(BF16)

    def proj(lo, width):
        return jnp.dot(h, w_ref[:, lo:lo + width], preferred_element_type=F32)

    mvt_ref[...] = _nt_dot(wvt_ref[0], h).astype(BF16)
    svt_ref[...] = _nt_dot(wvt_ref[1], h).astype(BF16)

    def norm_rope(t, t_sw, group, n, gain_ref, idx, cos, sin):
        r = lax.rsqrt(_group_sumsq(t, group) * (1.0 / n) + EPS)
        return (r * ((t * gain_ref[idx:idx + 1, :]) * cos + (t_sw * gain_ref[idx + 2:idx + 3, :]) * sin)).astype(BF16)

    cosa = jnp.concatenate([cosa_ref[...]] * 2, axis=1)
    sina = jnp.concatenate([sina_ref[...]] * 2, axis=1)
    for idx, dst in ((0, mq_ref), (1, mk_ref)):
        t = proj(C_MOBA + idx * GROUP_WIDTH, GROUP_WIDTH)
        t_sw = proj(C_MOBA + (idx + 2) * GROUP_WIDTH, GROUP_WIDTH)
        dst[...] = norm_rope(t, t_sw, HEAD_DIM, HEAD_DIM, gmoba_ref, idx, cosa, sina)

    cq = proj(C_CQ, 256)
    cq = (cq * lax.rsqrt(jnp.sum(cq * cq, axis=-1, keepdims=True) * (1.0 / MLA_Q_RANK) + EPS)
          * gcq_ref[...]).astype(BF16)
    ckv = proj(C_CKV, LANES)
    ckv = (ckv * lax.rsqrt(jnp.mean(ckv * ckv, axis=-1, keepdims=True) + EPS) * gckv_ref[...]).astype(BF16)
    krdt = proj(C_KR, LANES)
    dt_ref[...] = krdt
    lane = lax.broadcasted_iota(jnp.int32, krdt.shape, 1)
    kr = jnp.where((lane >= KR_LANE) & (lane < KR_LANE + MLA_ROPE), krdt, 0.0)
    cosm = jnp.concatenate([cosm_ref[...]] * N_HEADS, axis=1)
    sinm = jnp.concatenate([sinm_ref[...]] * N_HEADS, axis=1)
    ql = jnp.dot(cq, wuq_ref[0], preferred_element_type=F32)
    ql_sw = jnp.dot(cq, wuq_ref[1], preferred_element_type=F32)
    kl = jnp.dot(ckv, wuk_ref[...], preferred_element_type=F32) + jnp.concatenate([kr] * N_HEADS, axis=1)
    kl_sw = jnp.concatenate([pltpu.roll(krdt, KR_LANE, 1)] * N_HEADS, axis=1)
    lq_ref[...] = norm_rope(ql, ql_sw, LANES, MLA_QK, gmla_ref, 0, cosm, sinm)
    lk_ref[...] = norm_rope(kl, kl_sw, LANES, MLA_QK, gmla_ref, 1, cosm, sinm)
    lvt_ref[...] = _nt_dot(wuvt_ref[...], ckv).astype(BF16)

    sq_ref[...] = (proj(C_SB, GROUP_WIDTH) * (HEAD_DIM ** -0.5)).astype(BF16)
    sk_ref[...] = proj(C_SB + GROUP_WIDTH, GROUP_WIDTH).astype(BF16)

    z_ref[...] = proj(C_Z, GROUP_WIDTH)
    xbc_ref[...] = proj(C_XBC, SSM_XBC)


def _inproj(x2d, prm):
    t = x2d.shape[0]
    tm = TM_PROJ
    row = lambda w: pl.BlockSpec((tm, w), lambda i: (i, 0))
    tok = lambda w, dt: (jax.ShapeDtypeStruct((t, w), dt), row(w))
    tr = (jax.ShapeDtypeStruct((GROUP_WIDTH, t), BF16), pl.BlockSpec((GROUP_WIDTH, tm), lambda i: (0, i)))
    outs = [tok(256, BF16), tok(256, BF16), tr, tok(512, BF16), tok(512, BF16), tr,
            tok(256, BF16), tok(256, BF16), tr, tok(256, F32), tok(512, F32), tok(128, F32)]
    out_shapes = [o[0] for o in outs]
    out_specs = [o[1] for o in outs]
    consts = [prm["mix_g"], prm["w_in"], prm["w_vt"]]
    tables = [prm["rope"]] * 4
    table_specs = [pl.BlockSpec((tm, LANES), functools.partial(lambda j, i: (i, j), j)) for j in range(4)]
    tail = [prm["gmoba"], prm["gcq"], prm["gckv"], prm["wuq"], prm["wuk"], prm["wuvt"], prm["gmla"]]
    in_specs = ([row(D_MODEL)] + [_full(a.shape) for a in consts] + table_specs
                + [_full(a.shape) for a in tail])
    return pl.pallas_call(
        _inproj_body, grid=(t // tm,), in_specs=in_specs, out_specs=out_specs, out_shape=out_shapes,
        compiler_params=_params("arbitrary"), name="inproj",
    )(x2d, *consts, *tables, *tail)


QW = N_HEADS * TQ


def _head_slices(a):
    return [a[:, hd * TQ:(hd + 1) * TQ] for hd in range(N_HEADS)]


def _stack_heads(q):
    head_of_lane = lax.broadcasted_iota(jnp.int32, q.shape, 1) >> 6
    return jnp.concatenate([jnp.where(head_of_lane == hd, q, jnp.zeros_like(q)) for hd in range(N_HEADS)],
                           axis=0)


ONES_ROWS = 16


def _pv(vt_ref, k0, p, with_sum):
    ps = _head_slices(p.astype(BF16))
    outs = []
    for hd in range(N_HEADS):
        lhs = vt_ref[hd * HEAD_DIM:(hd + 1) * HEAD_DIM, pl.ds(k0, TQ)]
        if with_sum:
            r = lax.broadcasted_iota(jnp.int32, (ONES_ROWS, TQ), 0)
            lhs = jnp.concatenate([lhs, jnp.where(r == 0, 1.0, 0.0).astype(BF16)], axis=0)
        outs.append(jnp.dot(lhs, ps[hd], preferred_element_type=F32))
    return outs


def _key_query_iotas():
    key = lax.broadcasted_iota(jnp.int32, (TQ, QW), 0)
    qry = lax.broadcasted_iota(jnp.int32, (TQ, QW), 1) & (TQ - 1)
    return key, qry


def _softmax_init():
    return (jnp.full((1, QW), MASK_VALUE, F32), [jnp.zeros((HEAD_DIM + ONES_ROWS, TQ), F32)] * N_HEADS)


def _softmax_step(s, vt_ref, k0, m_run, acc, keep=None):
    m_new = jnp.maximum(m_run, jnp.max(s, axis=0, keepdims=True))
    m_sub = m_new
    if keep is not None:
        m_new = jnp.where(keep, m_new, m_run)
        m_sub = jnp.where(keep, m_new, -MASK_VALUE)
    al = _head_slices(jnp.exp2(m_run - m_new))
    pv = _pv(vt_ref, k0, jnp.exp2(s - m_sub), True)
    return m_new, [a * al[hd] + x for hd, (a, x) in enumerate(zip(acc, pv))]


ATTN_ROWS = 2


class _Chain:
    def __init__(self, s_a, s_b, scores, step, last=None):
        self.s_a, self.s_b, self.scores, self.step, self.last = s_a, s_b, scores, step, last


def _pipelined(i, chains, block_of, carries):
    def fetch(c, n):
        return c.scores(jnp.clip(block_of(n), 0, i))

    for c in chains:
        c.s_a[...] = fetch(c, 0)

    def pair(t, carries):
        n0 = 2 * t
        for c in chains:
            c.s_b[...] = fetch(c, n0 + 1)
        carries = [c.step(c.s_a[...], block_of(n0), cr) for c, cr in zip(chains, carries)]
        for c in chains:
            c.s_a[...] = fetch(c, n0 + 2)
        return [c.step(c.s_b[...], block_of(n0 + 1), cr) for c, cr in zip(chains, carries)]

    carries = lax.fori_loop(0, lax.shift_right_logical(i, 1), pair, carries)

    def tail_even(carries):
        return [cr if c.last is None else c.last(c.s_a[...], cr) for c, cr in zip(chains, carries)]

    def tail_odd(carries):
        for c in chains:
            if c.last is not None:
                c.s_b[...] = fetch(c, i)
        carries = [c.step(c.s_a[...], block_of(i - 1), cr) for c, cr in zip(chains, carries)]
        return [cr if c.last is None else c.last(c.s_b[...], cr) for c, cr in zip(chains, carries)]

    return lax.cond((i & 1) == 1, tail_odd, tail_even, carries)


def _write_heads(o_ref, row, acc, gain_ref, normalise):
    outs = []
    for hd in range(N_HEADS):
        o = acc[hd][:HEAD_DIM]
        if normalise:
            o = o * (1.0 / acc[hd][HEAD_DIM:HEAD_DIM + 1])
        ms = jnp.mean(o * o, axis=0, keepdims=True)
        outs.append(o * lax.rsqrt(ms + EPS) * gain_ref[hd * HEAD_DIM:(hd + 1) * HEAD_DIM, :])
    o_ref[row] = jnp.concatenate(outs, axis=0).T.astype(BF16)


def _key_col(row, blk):
    return pl.multiple_of(row * SEQ + blk * TQ, TQ)


def _moba_body(q_ref, k_ref, vt_ref, gout_ref, o_ref, sa_ref, sb_ref, kmean_ref, sel_ref):
    i = pl.program_id(1)
    nblk = SEQ // MOBA_BLOCK

    @pl.when(i == 0)
    def _():
        kmean_ref[...] = jnp.zeros(kmean_ref.shape, F32)
        for row in range(ATTN_ROWS):
            for n in range(nblk):
                kb = k_ref[row, n * MOBA_BLOCK:(n + 1) * MOBA_BLOCK, :].astype(F32)
                kmean_ref[row, n:n + 1, :] = jnp.mean(kb, axis=0, keepdims=True)

    key, qry = _key_query_iotas()

    def chain(row):
        qs = _stack_heads(q_ref[row])

        km = kmean_ref[row]
        km_hi = km.astype(BF16)
        km_lo = (km - km_hi.astype(F32)).astype(BF16)
        gate = _nt_dot(km_hi, qs) + _nt_dot(km_lo, qs)
        blk = lax.broadcasted_iota(jnp.int32, gate.shape, 0)
        rank = jnp.zeros(gate.shape, F32)
        for m in range(nblk - 1):
            gm = gate[m:m + 1, :]
            beats = (gm > gate) | ((gm == gate) & (blk > m))
            rank = rank + jnp.where(beats & (i > m), 1.0, 0.0)
        sel_ref[row] = jnp.where((blk < i) & (rank < float(MOBA_TOPK)), 1.0, 0.0)

        def scores(blk):
            return _nt_dot(k_ref[row, pl.ds(pl.multiple_of(blk * TQ, TQ), TQ), :], qs)

        def step(s, blk, carry):
            return _softmax_step(s, vt_ref, _key_col(row, blk), *carry, keep=sel_ref[row, pl.ds(blk, 1), :] > 0.5)

        def diagonal(s, carry):
            return _softmax_step(jnp.where(key <= qry, s, MASK_VALUE), vt_ref, _key_col(row, i), *carry)

        return _Chain(sa_ref.at[row], sb_ref.at[row], scores, step, diagonal)

    chains = [chain(row) for row in range(ATTN_ROWS)]
    carries = _pipelined(i, chains, lambda n: n, [_softmax_init()] * ATTN_ROWS)
    for row, (_, acc) in enumerate(carries):
        _write_heads(o_ref, row, acc, gout_ref, True)


def _attn_call(body, name, q, k, vt, extra, scratch):
    b, _, qw = q.shape
    kw = k.shape[-1]
    rows = ATTN_ROWS
    assert b % rows == 0
    return pl.pallas_call(
        body, grid=(b // rows, SEQ // TQ),
        in_specs=[pl.BlockSpec((rows, TQ, qw), lambda g, i: (g, i, 0)),
                  pl.BlockSpec((rows, SEQ, kw), lambda g, i: (g, 0, 0)),
                  pl.BlockSpec((GROUP_WIDTH, rows * SEQ), lambda g, i: (0, g))]
                 + [_full(a.shape) for a in extra],
        out_specs=pl.BlockSpec((rows, TQ, GROUP_WIDTH), lambda g, i: (g, i, 0)),
        out_shape=jax.ShapeDtypeStruct((b, SEQ, GROUP_WIDTH), BF16),
        scratch_shapes=[pltpu.VMEM((rows, TQ, QW), F32), pltpu.VMEM((rows, TQ, QW), F32)] + scratch,
        compiler_params=_params("arbitrary", "arbitrary"), name=name,
    )(q, k, vt, *extra)


def _moba(q, k, vt, gout):
    scratch = [pltpu.VMEM((ATTN_ROWS, 16, 256), F32), pltpu.VMEM((ATTN_ROWS, 16, QW), F32)]
    return _attn_call(_moba_body, "moba", q, k, vt, [gout], scratch)


def _mla_body(q_ref, k_ref, vt_ref, gout_ref, o_ref, sa_ref, sb_ref):
    i = pl.program_id(1)
    key, qry = _key_query_iotas()

    def chain(row):
        qh = [q_ref[row, :, hd * LANES:(hd + 1) * LANES] for hd in range(N_HEADS)]

        def scores(blk):
            k0 = pl.multiple_of(blk * TQ, TQ)
            return jnp.concatenate(
                [_nt_dot(k_ref[row, pl.ds(k0, TQ), hd * LANES:(hd + 1) * LANES], qh[hd])
                 for hd in range(N_HEADS)], axis=1)

        def step(s, blk, carry):
            return _softmax_step(s, vt_ref, _key_col(row, blk), *carry)

        def diagonal(s, carry):
            return _softmax_step(jnp.where(key <= qry, s, MASK_VALUE), vt_ref, _key_col(row, i), *carry)

        return _Chain(sa_ref.at[row], sb_ref.at[row], scores, step, diagonal)

    chains = [chain(row) for row in range(ATTN_ROWS)]
    carries = _pipelined(i, chains, lambda n: n, [_softmax_init()] * ATTN_ROWS)
    for row, (_, acc) in enumerate(carries):
        _write_heads(o_ref, row, acc, gout_ref, True)


def _mla(q, k, vt, gout):
    return _attn_call(_mla_body, "mla", q, k, vt, [gout], [])


def _sb_body(q_ref, k_ref, vt_ref, uu_ref, gout_ref, o_ref, sa_ref, sb_ref):
    i = pl.program_id(1)
    key, qry = _key_query_iotas()
    strict = key < qry

    def neg_log_survive(z):
        return jnp.maximum(z, 0.0) + jnp.log(1.0 + jnp.exp2(jnp.abs(z) * (-LOG2E)))

    def suffix_sums(c):
        hi = c.astype(BF16)
        lo = (c - hi.astype(F32)).astype(BF16)
        t = jnp.dot(uu_ref[...], jnp.concatenate([hi, lo], axis=0), preferred_element_type=F32)
        return t[:TQ], t[TQ:TQ + 1]

    def chain(row):
        qs = _stack_heads(q_ref[row])

        def scores(blk):
            return _nt_dot(k_ref[row, pl.ds(pl.multiple_of(blk * TQ, TQ), TQ), :], qs)

        def step(z, blk, carry, diagonal=False):
            csum, acc = carry
            c = neg_log_survive(z)
            if diagonal:
                c = jnp.where(strict, c, 0.0)
            incl, total = suffix_sums(c)
            a = jnp.exp(z - (incl + csum))
            if diagonal:
                a = jnp.where(strict, a, 0.0)
            pv = _pv(vt_ref, _key_col(row, blk), a, False)
            return csum + total, [x + y for x, y in zip(acc, pv)]

        return _Chain(sa_ref.at[row], sb_ref.at[row], scores, step)

    chains = [chain(row) for row in range(ATTN_ROWS)]
    zero = (jnp.zeros((1, QW), F32), [jnp.zeros((HEAD_DIM, TQ), F32)] * N_HEADS)
    carries = [c.step(c.scores(i), i, zero, diagonal=True) for c in chains]
    carries = _pipelined(i, chains, lambda n: i - 1 - n, carries)
    for row, (_, acc) in enumerate(carries):
        _write_heads(o_ref, row, acc, gout_ref, False)


SB_SUM_ROWS = TQ + 16


def _sb(q, k, vt, gout):
    r = lax.broadcasted_iota(jnp.int32, (SB_SUM_ROWS, 2 * TQ), 0)
    c = lax.broadcasted_iota(jnp.int32, (SB_SUM_ROWS, 2 * TQ), 1) & (TQ - 1)
    uu = jnp.where((c >= r) | (r == TQ), 1.0, 0.0).astype(BF16)
    return _attn_call(_sb_body, "stickbreak", q, k, vt, [uu, gout], [])


def _ssd_body(z_ref, xbc_ref, dt_ref, cw_ref, cb_ref, dtb_ref, alog_ref, dskip_ref, ng_ref, o_ref,
              halo_ref, st_ref):
    c = pl.program_id(1)
    cl = SSM_CHUNK

    @pl.when(c == 0)
    def _():
        halo_ref[...] = jnp.zeros(halo_ref.shape, F32)
        st_ref[...] = jnp.zeros(st_ref.shape, F32)

    for row in range(SSD_ROWS):
        _ssd_chunk(row, z_ref, xbc_ref, dt_ref, cw_ref, cb_ref, dtb_ref, alog_ref, dskip_ref, ng_ref,
                   o_ref, halo_ref, st_ref)


def _ssd_chunk(row, z_ref, xbc_ref, dt_ref, cw_ref, cb_ref, dtb_ref, alog_ref, dskip_ref, ng_ref, o_ref,
               halo_ref, st_ref):
    cl = SSM_CHUNK
    xb = xbc_ref[row]
    ext = jnp.concatenate([halo_ref[row], xb], axis=0)
    u = cb_ref[...] + cw_ref[SSM_CONV - 1:SSM_CONV, :] * xb
    for tap in range(SSM_CONV - 1):
        shift = SSM_CONV - 1 - tap
        u = u + cw_ref[tap:tap + 1, :] * pltpu.roll(ext, shift, 0)[8:8 + cl]
    halo_ref[row] = xb[cl - 8:cl]
    xc = u * _sigmoid(u)
    xs = xc[:, 0:GROUP_WIDTH]
    bm = xc[:, GROUP_WIDTH:GROUP_WIDTH + LANES]
    cm = xc[:, GROUP_WIDTH + LANES:SSM_XBC]

    lane = lax.broadcasted_iota(jnp.int32, (cl, LANES), 1)
    head_lane = (lane >= DT_LANE) & (lane < DT_LANE + SSM_HEADS)
    dtv = dt_ref[row] + dtb_ref[...]
    dt = jnp.where(head_lane, jnp.maximum(dtv, 0.0) + _softplus_neg_abs(dtv), 0.0)
    a = dt * (-jnp.exp(alog_ref[...]))

    r = lax.broadcasted_iota(jnp.int32, (cl, cl), 0)
    s = lax.broadcasted_iota(jnp.int32, (cl, cl), 1)
    tril = s <= r
    tril_b = jnp.where(tril, 1.0, 0.0).astype(BF16)
    a1 = a.astype(BF16)
    a2 = (a - a1.astype(F32)).astype(BF16)
    a3 = (a - a1.astype(F32) - a2.astype(F32)).astype(BF16)
    acum = (jnp.dot(tril_b, a1, preferred_element_type=F32)
            + (jnp.dot(tril_b, a2, preferred_element_type=F32)
               + jnp.dot(tril_b, a3, preferred_element_type=F32)))
    acum_t = acum.T

    lane2 = lax.broadcasted_iota(jnp.int32, (cl, GROUP_WIDTH), 1)
    head2 = lane2 >> 6

    def per_head(v):
        out = jnp.broadcast_to(v[:, DT_LANE + 3:DT_LANE + 4], (cl, GROUP_WIDTH))
        for hd in (2, 1, 0):
            out = jnp.where(head2 == hd, v[:, DT_LANE + hd:DT_LANE + hd + 1], out)
        return out

    dt_full = per_head(dt)
    acum_full = per_head(acum)
    last_full = acum_full[cl - 1:cl, :]
    xdt = xs * dt_full
    xdt_b = xdt.astype(BF16)

    grp = lane >> 6
    bm_b = bm.astype(BF16)
    y = jnp.zeros((cl, GROUP_WIDTH), F32)
    for g in range(2):
        cg = jnp.where(grp == g, cm, 0.0).astype(BF16)
        gmat = _nt_dot(cg, bm_b)
        for hd in (2 * g, 2 * g + 1):
            seg = acum[:, DT_LANE + hd:DT_LANE + hd + 1] - acum_t[DT_LANE + hd:DT_LANE + hd + 1, :]
            lmat = jnp.where(tril, jnp.exp(jnp.where(tril, seg, 0.0)), 0.0)
            yd = jnp.dot((gmat * lmat).astype(BF16), xdt_b, preferred_element_type=F32)
            y = jnp.where(head2 == hd, yd, y)

    st_prev = st_ref[row]
    y_off = jnp.exp(acum_full) * jnp.dot(cm.astype(BF16), st_prev.astype(BF16), preferred_element_type=F32)
    dec = jnp.exp(last_full - acum_full)
    s_new = jnp.dot(bm.T.astype(BF16), (xdt * dec).astype(BF16), preferred_element_type=F32)
    rs = lax.broadcasted_iota(jnp.int32, (LANES, GROUP_WIDTH), 0) >> 6
    cs = lax.broadcasted_iota(jnp.int32, (LANES, GROUP_WIDTH), 1) >> 7
    st_ref[row] = jnp.exp(last_full) * st_prev + jnp.where(rs == cs, s_new, 0.0)

    y = y + y_off + dskip_ref[...] * xs
    zz = z_ref[row]
    y = y * (zz * _sigmoid(zz))
    ms = jnp.mean(y * y, axis=-1, keepdims=True)
    o_ref[row] = (y * lax.rsqrt(ms + EPS) * ng_ref[...]).astype(BF16)


SSD_ROWS = 4


def _ssd(z, xbc, dt, prm):
    b = z.shape[0]
    cl = SSM_CHUNK
    rows = SSD_ROWS
    assert b % rows == 0
    small = [prm["ssm_cw"], prm["ssm_cb"], prm["ssm_dtb"], prm["ssm_alog"], prm["ssm_d"], prm["ssm_ng"]]
    return pl.pallas_call(
        _ssd_body, grid=(b // rows, SEQ // cl),
        in_specs=[pl.BlockSpec((rows, cl, 256), lambda g, c: (g, c, 0)),
                  pl.BlockSpec((rows, cl, 512), lambda g, c: (g, c, 0)),
                  pl.BlockSpec((rows, cl, LANES), lambda g, c: (g, c, 0))]
                 + [_full(a.shape) for a in small],
        out_specs=pl.BlockSpec((rows, cl, 256), lambda g, c: (g, c, 0)),
        out_shape=jax.ShapeDtypeStruct((b, SEQ, 256), BF16),
        scratch_shapes=[pltpu.VMEM((rows, 8, SSM_XBC), F32), pltpu.VMEM((rows, LANES, GROUP_WIDTH), F32)],
        compiler_params=_params("arbitrary", "arbitrary"), name="ssd",
    )(z, xbc, dt, *small)


def _ffn_body(x_ref, o1_ref, o2_ref, o3_ref, o4_ref, xh_ref, oh1_ref, oh2_ref, oh3_ref, oh4_ref,
              wmix_ref, g_ref, win_ref, cw_ref, cb_ref, wout_ref, y_ref, act_ref):
    tm = TM_PROJ
    first = (pl.program_id(0) % (SEQ // tm)) == 0
    g = g_ref[...]

    def mix(xr, o_refs):
        acc = xr[...]
        for n, o_ref in enumerate(o_refs):
            acc = acc + jnp.dot(o_ref[...], wmix_ref[n * GROUP_WIDTH:(n + 1) * GROUP_WIDTH, :],
                                preferred_element_type=F32)
        return acc

    def norm(v):
        ms = jnp.mean(v * v, axis=-1, keepdims=True)
        return v * lax.rsqrt(ms + EPS) * g

    x = mix(x_ref, (o1_ref, o2_ref, o3_ref, o4_ref))
    x_halo = mix(xh_ref, (oh1_ref, oh2_ref, oh3_ref, oh4_ref))
    h_halo = jnp.where(first, 0.0, norm(x_halo)).astype(BF16)
    h_ext = jnp.concatenate([h_halo, norm(x).astype(BF16)], axis=0)

    def conv(u, lo, width):
        w = cw_ref[:, lo:lo + width]
        out = cb_ref[:, lo:lo + width] + w[2:3, :] * u[FFN_HALO:, :]
        out = out + w[1:2, :] * pltpu.roll(u, 1, 0)[FFN_HALO:, :]
        return out + w[0:1, :] * pltpu.roll(u, 2, 0)[FFN_HALO:, :]

    for lo, width in FFN_CHUNKS:
        gate = conv(jnp.dot(h_ext, win_ref[:, lo:lo + width], preferred_element_type=F32), lo, width)
        up = conv(jnp.dot(h_ext, win_ref[:, D_FF + lo:D_FF + lo + width], preferred_element_type=F32),
                  D_FF + lo, width)
        act_ref[:, lo:lo + width] = (gate * _sigmoid(gate) * up).astype(BF16)
    y_ref[...] = x + jnp.dot(act_ref[...], wout_ref[...], preferred_element_type=F32)


def _mix_ffn(x2d, outs, prm):
    t = x2d.shape[0]
    tm = TM_PROJ
    per = tm // FFN_HALO
    consts = [prm["w_out"], prm["ffn_g"], prm["ffn_win"], prm["ffn_cw"], prm["ffn_cb"], prm["ffn_wout"]]
    main = lambda w: pl.BlockSpec((tm, w), lambda i: (i, 0))
    halo = lambda w: pl.BlockSpec((FFN_HALO, w), lambda i: (jnp.maximum(i * per - 1, 0), 0))
    return pl.pallas_call(
        _ffn_body, grid=(t // tm,),
        in_specs=[main(D_MODEL)] + [main(GROUP_WIDTH)] * 4 + [halo(D_MODEL)] + [halo(GROUP_WIDTH)] * 4
                 + [_full(a.shape) for a in consts],
        out_specs=pl.BlockSpec((tm, D_MODEL), lambda i: (i, 0)),
        out_shape=jax.ShapeDtypeStruct((t, D_MODEL), F32),
        scratch_shapes=[pltpu.VMEM((tm, D_FF), BF16)],
        compiler_params=_params("arbitrary"), name="mix_ffn",
    )(x2d, *outs, x2d, *outs, *consts)


def _rope_tables(positions):
    pos = positions.reshape(-1).astype(F32)[:, None]
    inv_a = 1.0 / (ROPE_THETA ** (jnp.arange(0, HEAD_DIM, 2, dtype=F32) / HEAD_DIM))
    inv_m = 1.0 / (ROPE_THETA ** (jnp.arange(0, MLA_ROPE, 2, dtype=F32) / MLA_ROPE))
    ang = pos * jnp.concatenate([inv_a, inv_m])
    cos_all, sin_all = lax.optimization_barrier((jnp.cos(ang), jnp.sin(ang)))
    na, nm = HEAD_DIM // 2, MLA_ROPE // 2
    n_ang = na + nm
    sel = np.zeros((LANES, 4 * LANES), np.float32)
    for lane in range(LANES):
        sel[lane % na, lane] = 1.0
        sel[n_ang + lane % na, LANES + lane] = -1.0 if lane % HEAD_DIM < na else 1.0
        r = lane - MLA_NOPE
        if 0 <= r < MLA_ROPE:
            sel[na + r % nm, 2 * LANES + lane] = 1.0
            sel[n_ang + na + r % nm, 3 * LANES + lane] = -1.0 if r < nm else 1.0
        else:
            sel[2 * n_ang, 2 * LANES + lane] = 1.0
    n = pos.shape[0]
    feats = jnp.concatenate([cos_all, sin_all, jnp.ones_like(pos), jnp.zeros((n, LANES - 2 * n_ang - 1), F32)],
                            axis=1)
    return pl.pallas_call(
        _rope_expand_body, grid=(n // ROPE_TILE,),
        in_specs=[pl.BlockSpec((ROPE_TILE, LANES), lambda i: (i, 0)), _full(sel.shape)],
        out_specs=pl.BlockSpec((ROPE_TILE, 4 * LANES), lambda i: (i, 0)),
        out_shape=jax.ShapeDtypeStruct((n, 4 * LANES), F32),
        compiler_params=_params("arbitrary"), name="rope_tables",
    )(feats, jnp.asarray(sel, BF16))


ROPE_TILE = 2048


def _rope_expand_body(f_ref, sel_ref, o_ref):
    f = f_ref[...]
    f1 = f.astype(BF16)
    r1 = f - f1.astype(F32)
    f2 = r1.astype(BF16)
    f3 = (r1 - f2.astype(F32)).astype(BF16)
    sel = sel_ref[...]
    dot = lambda a: jnp.dot(a, sel, preferred_element_type=F32)
    o_ref[...] = dot(f1) + (dot(f2) + dot(f3))


def _pad_heads(w, per_head, width):
    lead = w.shape[:-1]
    w = w.reshape(*lead, N_HEADS, per_head)
    w = jnp.pad(w, [(0, 0)] * len(lead) + [(0, 0), (0, width - per_head)])
    return w.reshape(*lead, N_HEADS * width)


def _layer_params(l, mix_norm_g, w_in, moba_qk_g, mla_q_norm_g, mla_kv_norm_g, mla_w_uq, mla_w_ukv,
                  mla_qk_g, ssm_conv_w, ssm_conv_b, ssm_dt_bias, ssm_a_log, ssm_d, ssm_norm_g,
                  head_out_g, w_out, ffn_norm_g, ffn_w_in, ffn_conv_w, ffn_conv_b, ffn_w_out):
    w = w_in[l].astype(BF16)
    o = 0
    moba_w = w[:, o:o + 768]; o += 768
    cq_w = w[:, o:o + MLA_Q_RANK]; o += MLA_Q_RANK
    ckv_w = w[:, o:o + MLA_KV_RANK]; o += MLA_KV_RANK
    kr_w = w[:, o:o + MLA_ROPE]; o += MLA_ROPE
    sb_w = w[:, o:o + 768]; o += 768
    z_w = w[:, o:o + 256]; o += 256
    xbc_w = w[:, o:o + SSM_XBC]; o += SSM_XBC
    dt_w = w[:, o:o + SSM_HEADS]
    zc = lambda n: jnp.zeros((D_MODEL, n), BF16)
    half = HEAD_DIM // 2
    swap_a = (jnp.arange(GROUP_WIDTH) // HEAD_DIM) * HEAD_DIM + (jnp.arange(GROUP_WIDTH) % HEAD_DIM + half) % HEAD_DIM
    rh = MLA_ROPE // 2
    swap_r = jnp.concatenate([jnp.arange(rh, MLA_ROPE), jnp.arange(0, rh)])
    in_head = jnp.concatenate([jnp.arange(MLA_NOPE), MLA_NOPE + swap_r, jnp.arange(MLA_QK, LANES)])
    swap_m = (jnp.arange(N_HEADS)[:, None] * LANES + in_head[None, :]).reshape(-1)
    w_r = jnp.concatenate([
        moba_w[:, :512], moba_w[:, :256][:, swap_a], moba_w[:, 256:512][:, swap_a],
        cq_w, zc(256 - MLA_Q_RANK), ckv_w,
        kr_w[:, swap_r], zc(KR_LANE - MLA_ROPE), kr_w, dt_w, zc(LANES - DT_LANE - SSM_HEADS),
        sb_w[:, :512], z_w, xbc_w], axis=1)
    w_vt = jnp.stack([moba_w[:, 512:].T, sb_w[:, 512:].T])

    lane_row = lambda v, start: jnp.zeros((1, LANES), F32).at[0, start:start + v.shape[0]].set(v)
    moba_scale = jnp.array([[HEAD_DIM ** -0.5 * LOG2E], [1.0]], F32)
    mla_scale = jnp.array([[MLA_QK ** -0.5 * LOG2E], [1.0]], F32)
    gmoba = jnp.tile(moba_qk_g[l], (1, N_HEADS)) * moba_scale
    gmla = jnp.tile(jnp.pad(mla_qk_g[l], ((0, 0), (0, LANES - MLA_QK))), (1, N_HEADS)) * mla_scale
    wuq = jnp.pad(_pad_heads(mla_w_uq[l], MLA_QK, LANES), ((0, 256 - MLA_Q_RANK), (0, 0)))
    ukv = mla_w_ukv[l].reshape(MLA_KV_RANK, N_HEADS, MLA_NOPE + HEAD_DIM)
    hog = head_out_g[l].reshape(3, GROUP_WIDTH, 1)
    return {
        "mix_g": mix_norm_g[l][None, :], "w_in": w_r, "w_vt": w_vt,
        "gmoba": jnp.concatenate([gmoba, gmoba[:, swap_a]], axis=0),
        "gcq": jnp.pad(mla_q_norm_g[l], (0, 256 - MLA_Q_RANK))[None, :],
        "gckv": mla_kv_norm_g[l][None, :],
        "wuq": jnp.stack([wuq, wuq[:, swap_m]]).astype(BF16),
        "wuk": _pad_heads(ukv[:, :, :MLA_NOPE].reshape(MLA_KV_RANK, -1), MLA_NOPE, LANES).astype(BF16),
        "wuvt": ukv[:, :, MLA_NOPE:].reshape(MLA_KV_RANK, GROUP_WIDTH).T.astype(BF16),
        "gmla": jnp.concatenate([gmla, gmla[:, swap_m]], axis=0),
        "ssm_cw": ssm_conv_w[l], "ssm_cb": ssm_conv_b[l][None, :],
        "ssm_dtb": lane_row(ssm_dt_bias[l], DT_LANE), "ssm_alog": lane_row(ssm_a_log[l], DT_LANE),
        "ssm_d": jnp.repeat(ssm_d[l], HEAD_DIM)[None, :], "ssm_ng": ssm_norm_g[l][None, :],
        "g_moba_out": hog[0], "g_mla_out": hog[1], "g_sb_out": hog[2],
        "w_out": w_out[l].astype(BF16),
        "ffn_g": ffn_norm_g[l][None, :], "ffn_win": ffn_w_in[l].astype(BF16),
        "ffn_cw": ffn_conv_w[l], "ffn_cb": ffn_conv_b[l][None, :], "ffn_wout": ffn_w_out[l].astype(BF16),
    }


def _layer(x2d, bsz, prm):
    (mq, mk, mvt, lq, lk, lvt, sq, sk, svt, z, xbc, dt) = _inproj(x2d, prm)
    r3 = lambda a: a.reshape(bsz, SEQ, a.shape[-1])
    o_moba = _moba(r3(mq), r3(mk), mvt, prm["g_moba_out"])
    o_mla = _mla(r3(lq), r3(lk), lvt, prm["g_mla_out"])
    o_sb = _sb(r3(sq), r3(sk), svt, prm["g_sb_out"])
    o_ssm = _ssd(r3(z), r3(xbc), r3(dt), prm)
    flat = lambda a: a.reshape(bsz * SEQ, GROUP_WIDTH)
    return _mix_ffn(x2d, [flat(o_moba), flat(o_mla), flat(o_sb), flat(o_ssm)], prm)


def kernel(x, positions, mix_norm_g, w_in, moba_qk_g, mla_q_norm_g, mla_kv_norm_g, mla_w_uq, mla_w_ukv,
           mla_qk_g, ssm_conv_w, ssm_conv_b, ssm_dt_bias, ssm_a_log, ssm_d, ssm_norm_g, head_out_g, w_out,
           ffn_norm_g, ffn_w_in, ffn_conv_w, ffn_conv_b, ffn_w_out):
    bsz, seq, d = x.shape
    assert seq == SEQ and d == D_MODEL
    rope = _rope_tables(positions)
    x2d = x.reshape(bsz * seq, d)
    for l in range(w_in.shape[0]):
        prm = _layer_params(l, mix_norm_g, w_in, moba_qk_g, mla_q_norm_g, mla_kv_norm_g, mla_w_uq,
                            mla_w_ukv, mla_qk_g, ssm_conv_w, ssm_conv_b, ssm_dt_bias, ssm_a_log, ssm_d,
                            ssm_norm_g, head_out_g, w_out, ffn_norm_g, ffn_w_in, ffn_conv_w, ffn_conv_b,
                            ffn_w_out)
        prm["rope"] = rope
        x2d = _layer(x2d, bsz, prm)
    return x2d.reshape(bsz, seq, d)
```

```python
import functools
import math

import numpy as np
import jax
import jax.numpy as jnp
from jax import lax
from jax.experimental import pallas as pl
from jax.experimental.pallas import tpu as pltpu

F32 = jnp.float32
BF16 = jnp.bfloat16

D_MODEL = 1024
SEQ = 2048
GROUP_WIDTH = 256
HEAD_DIM = 64
N_HEADS = 4
MOBA_BLOCK = 256
MOBA_TOPK = 3
MLA_Q_RANK = 192
MLA_KV_RANK = 128
MLA_NOPE = 64
MLA_ROPE = 32
MLA_QK = MLA_NOPE + MLA_ROPE
SSM_HEADS = 4
SSM_STATE = 64
SSM_CONV = 4
SSM_CHUNK = 128
SSM_XBC = 512
D_FF = 2816
FFN_CONV = 3
ROPE_THETA = 10000.0
EPS = 1e-6

LANES = 128
VMEM_LIMIT_BYTES = 56 * 1024 * 1024
MASK_VALUE = -1e30
LOG2E = math.log2(math.e)

C_MOBA = 0
C_CQ = 1024
C_CKV = 1280
C_KR = 1408
C_SB = 1536
C_Z = 2048
C_XBC = 2304
KR_LANE = 64
DT_LANE = 96

TM_PROJ = 512
TQ = 256
FFN_HALO = 16
FFN_CHUNKS = ((0, 512), (512, 512), (1024, 512), (1536, 512), (2048, 512), (2560, 256))


def _params(*sem):
    return pltpu.CompilerParams(dimension_semantics=sem, vmem_limit_bytes=VMEM_LIMIT_BYTES)


def _full(shape):
    n = len(shape)
    return pl.BlockSpec(shape, lambda *_: (0,) * n)


def _sigmoid(x):
    return 1.0 / (1.0 + jnp.exp(-x))


def _softplus_neg_abs(x):
    return jnp.log(1.0 + jnp.exp(-jnp.abs(x)))


def _group_sumsq(x, group):
    xx = x * x
    outs = []
    for c in range(x.shape[1] // LANES):
        slab = xx[:, c * LANES:(c + 1) * LANES]
        if group == LANES:
            s = jnp.sum(slab, axis=1, keepdims=True)
            outs.append(jnp.broadcast_to(s, slab.shape))
        else:
            lane = lax.broadcasted_iota(jnp.int32, slab.shape, 1)
            lo = lane < group
            s_lo = jnp.sum(jnp.where(lo, slab, 0.0), axis=1, keepdims=True)
            s_hi = jnp.sum(jnp.where(lo, 0.0, slab), axis=1, keepdims=True)
            outs.append(jnp.where(lo, s_lo, s_hi))
    return outs[0] if len(outs) == 1 else jnp.concatenate(outs, axis=1)


def _nt_dot(a, b):
    return lax.dot_general(a, b, (((1,), (1,)), ((), ())), preferred_element_type=F32)


def _inproj_body(x_ref, g_ref, w_ref, wvt_ref, cosa_ref, sina_ref, cosm_ref, sinm_ref, gmoba_ref, gcq_ref,
                 gckv_ref, wuq_ref, wuk_ref, wuvt_ref, gmla_ref,
                 mq_ref, mk_ref, mvt_ref, lq_ref, lk_ref, lvt_ref, sq_ref, sk_ref, svt_ref,
                 z_ref, xbc_ref, dt_ref):
    x = x_ref[...]
    ms = jnp.mean(x * x, axis=-1, keepdims=True)
    h = (x * lax.rsqrt(ms + EPS) * g_ref[...]).astype(BF16)

    def proj(lo, width):
        return jnp.dot(h, w_ref[:, lo:lo + width], preferred_element_type=F32)

    mvt_ref[...] = _nt_dot(wvt_ref[0], h).astype(BF16)
    svt_ref[...] = _nt_dot(wvt_ref[1], h).astype(BF16)

    def norm_rope(t, t_sw, group, n, gain_ref, idx, cos, sin):
        r = lax.rsqrt(_group_sumsq(t, group) * (1.0 / n) + EPS)
        return (r * ((t * gain_ref[idx:idx + 1, :]) * cos + (t_sw * gain_ref[idx + 2:idx + 3, :]) * sin)).astype(BF16)

    cosa = jnp.concatenate([cosa_ref[...]] * 2, axis=1)
    sina = jnp.concatenate([sina_ref[...]] * 2, axis=1)
    for idx, dst in ((0, mq_ref), (1, mk_ref)):
        t = proj(C_MOBA + idx * GROUP_WIDTH, GROUP_WIDTH)
        t_sw = proj(C_MOBA + (idx + 2) * GROUP_WIDTH, GROUP_WIDTH)
        dst[...] = norm_rope(t, t_sw, HEAD_DIM, HEAD_DIM, gmoba_ref, idx, cosa, sina)

    cq = proj(C_CQ, 256)
    cq = (cq * lax.rsqrt(jnp.sum(cq * cq, axis=-1, keepdims=True) * (1.0 / MLA_Q_RANK) + EPS)
          * gcq_ref[...]).astype(BF16)
    ckv = proj(C_CKV, LANES)
    ckv = (ckv * lax.rsqrt(jnp.mean(ckv * ckv, axis=-1, keepdims=True) + EPS) * gckv_ref[...]).astype(BF16)
    krdt = proj(C_KR, LANES)
    dt_ref[...] = krdt
    lane = lax.broadcasted_iota(jnp.int32, krdt.shape, 1)
    kr = jnp.where((lane >= KR_LANE) & (lane < KR_LANE + MLA_ROPE), krdt, 0.0)
    cosm = jnp.concatenate([cosm_ref[...]] * N_HEADS, axis=1)
    sinm = jnp.concatenate([sinm_ref[...]] * N_HEADS, axis=1)
    ql = jnp.dot(cq, wuq_ref[0], preferred_element_type=F32)
    ql_sw = jnp.dot(cq, wuq_ref[1], preferred_element_type=F32)
    kl = jnp.dot(ckv, wuk_ref[...], preferred_element_type=F32) + jnp.concatenate([kr] * N_HEADS, axis=1)
    kl_sw = jnp.concatenate([pltpu.roll(krdt, KR_LANE, 1)] * N_HEADS, axis=1)
    lq_ref[...] = norm_rope(ql, ql_sw, LANES, MLA_QK, gmla_ref, 0, cosm, sinm)
    lk_ref[...] = norm_rope(kl, kl_sw, LANES, MLA_QK, gmla_ref, 1, cosm, sinm)
    lvt_ref[...] = _nt_dot(wuvt_ref[...], ckv).astype(BF16)

    sq_ref[...] = (proj(C_SB, GROUP_WIDTH) * (HEAD_DIM ** -0.5)).astype(BF16)
    sk_ref[...] = proj(C_SB + GROUP_WIDTH, GROUP_WIDTH).astype(BF16)

    z_ref[...] = proj(C_Z, GROUP_WIDTH)
    xbc_ref[...] = proj(C_XBC, SSM_XBC)


def _inproj(x2d, prm):
    t = x2d.shape[0]
    tm = TM_PROJ
    row = lambda w: pl.BlockSpec((tm, w), lambda i: (i, 0))
    tok = lambda w, dt: (jax.ShapeDtypeStruct((t, w), dt), row(w))
    tr = (jax.ShapeDtypeStruct((GROUP_WIDTH, t), BF16), pl.BlockSpec((GROUP_WIDTH, tm), lambda i: (0, i)))
    outs = [tok(256, BF16), tok(256, BF16), tr, tok(512, BF16), tok(512, BF16), tr,
            tok(256, BF16), tok(256, BF16), tr, tok(256, F32), tok(512, F32), tok(128, F32)]
    out_shapes = [o[0] for o in outs]
    out_specs = [o[1] for o in outs]
    consts = [prm["mix_g"], prm["w_in"], prm["w_vt"]]
    tables = [prm["rope"]] * 4
    table_specs = [pl.BlockSpec((tm, LANES), functools.partial(lambda j, i: (i, j), j)) for j in range(4)]
    tail = [prm["gmoba"], prm["gcq"], prm["gckv"], prm["wuq"], prm["wuk"], prm["wuvt"], prm["gmla"]]
    in_specs = ([row(D_MODEL)] + [_full(a.shape) for a in consts] + table_specs
                + [_full(a.shape) for a in tail])
    return pl.pallas_call(
        _inproj_body, grid=(t // tm,), in_specs=in_specs, out_specs=out_specs, out_shape=out_shapes,
        compiler_params=_params("arbitrary"), name="inproj",
    )(x2d, *consts, *tables, *tail)


QW = N_HEADS * TQ


def _head_slices(a):
    return [a[:, hd * TQ:(hd + 1) * TQ] for hd in range(N_HEADS)]


def _stack_heads(q):
    head_of_lane = lax.broadcasted_iota(jnp.int32, q.shape, 1) >> 6
    return jnp.concatenate([jnp.where(head_of_lane == hd, q, jnp.zeros_like(q)) for hd in range(N_HEADS)],
                           axis=0)


ONES_ROWS = 16


def _pv(vt_ref, k0, p, with_sum):
    ps = _head_slices(p.astype(BF16))
    outs = []
    for hd in range(N_HEADS):
        lhs = vt_ref[hd * HEAD_DIM:(hd + 1) * HEAD_DIM, pl.ds(k0, TQ)]
        if with_sum:
            r = lax.broadcasted_iota(jnp.int32, (ONES_ROWS, TQ), 0)
            lhs = jnp.concatenate([lhs, jnp.where(r == 0, 1.0, 0.0).astype(BF16)], axis=0)
        outs.append(jnp.dot(lhs, ps[hd], preferred_element_type=F32))
    return outs


def _key_query_iotas():
    key = lax.broadcasted_iota(jnp.int32, (TQ, QW), 0)
    qry = lax.broadcasted_iota(jnp.int32, (TQ, QW), 1) & (TQ - 1)
    return key, qry


def _softmax_init():
    return (jnp.full((1, QW), MASK_VALUE, F32), [jnp.zeros((HEAD_DIM + ONES_ROWS, TQ), F32)] * N_HEADS)


def _softmax_step(s, vt_ref, k0, m_run, acc, keep=None):
    m_new = jnp.maximum(m_run, jnp.max(s, axis=0, keepdims=True))
    m_sub = m_new
    if keep is not None:
        m_new = jnp.where(keep, m_new, m_run)
        m_sub = jnp.where(keep, m_new, -MASK_VALUE)
    al = _head_slices(jnp.exp2(m_run - m_new))
    pv = _pv(vt_ref, k0, jnp.exp2(s - m_sub), True)
    return m_new, [a * al[hd] + x for hd, (a, x) in enumerate(zip(acc, pv))]


ATTN_ROWS = 4


class _Chain:
    def __init__(self, s_a, s_b, scores, step, last=None):
        self.s_a, self.s_b, self.scores, self.step, self.last = s_a, s_b, scores, step, last


def _pipelined(i, chains, block_of, carries):
    def fetch(c, n):
        return c.scores(jnp.clip(block_of(n), 0, i))

    for c in chains:
        c.s_a[...] = fetch(c, 0)

    def pair(t, carries):
        n0 = 2 * t
        for c in chains:
            c.s_b[...] = fetch(c, n0 + 1)
        carries = [c.step(c.s_a[...], block_of(n0), cr) for c, cr in zip(chains, carries)]
        for c in chains:
            c.s_a[...] = fetch(c, n0 + 2)
        return [c.step(c.s_b[...], block_of(n0 + 1), cr) for c, cr in zip(chains, carries)]

    carries = lax.fori_loop(0, lax.shift_right_logical(i, 1), pair, carries)

    def tail_even(carries):
        return [cr if c.last is None else c.last(c.s_a[...], cr) for c, cr in zip(chains, carries)]

    def tail_odd(carries):
        for c in chains:
            if c.last is not None:
                c.s_b[...] = fetch(c, i)
        carries = [c.step(c.s_a[...], block_of(i - 1), cr) for c, cr in zip(chains, carries)]
        return [cr if c.last is None else c.last(c.s_b[...], cr) for c, cr in zip(chains, carries)]

    return lax.cond((i & 1) == 1, tail_odd, tail_even, carries)


def _write_heads(o_ref, row, acc, gain_ref, normalise):
    outs = []
    for hd in range(N_HEADS):
        o = acc[hd][:HEAD_DIM]
        if normalise:
            o = o * (1.0 / acc[hd][HEAD_DIM:HEAD_DIM + 1])
        ms = jnp.mean(o * o, axis=0, keepdims=True)
        outs.append(o * lax.rsqrt(ms + EPS) * gain_ref[hd * HEAD_DIM:(hd + 1) * HEAD_DIM, :])
    o_ref[row] = jnp.concatenate(outs, axis=0).T.astype(BF16)


def _key_col(row, blk):
    return pl.multiple_of(row * SEQ + blk * TQ, TQ)


def _moba_body(q_ref, k_ref, vt_ref, gout_ref, o_ref, sa_ref, sb_ref, kmean_ref, sel_ref):
    i = pl.program_id(1)
    nblk = SEQ // MOBA_BLOCK

    @pl.when(i == 0)
    def _():
        kmean_ref[...] = jnp.zeros(kmean_ref.shape, F32)
        for row in range(ATTN_ROWS):
            for n in range(nblk):
                kb = k_ref[row, n * MOBA_BLOCK:(n + 1) * MOBA_BLOCK, :].astype(F32)
                kmean_ref[row, n:n + 1, :] = jnp.mean(kb, axis=0, keepdims=True)

    key, qry = _key_query_iotas()

    def chain(row):
        qs = _stack_heads(q_ref[row])

        km = kmean_ref[row]
        km_hi = km.astype(BF16)
        km_lo = (km - km_hi.astype(F32)).astype(BF16)
        gate = _nt_dot(km_hi, qs) + _nt_dot(km_lo, qs)
        blk = lax.broadcasted_iota(jnp.int32, gate.shape, 0)
        rank = jnp.zeros(gate.shape, F32)
        for m in range(nblk - 1):
            gm = gate[m:m + 1, :]
            beats = (gm > gate) | ((gm == gate) & (blk > m))
            rank = rank + jnp.where(beats & (i > m), 1.0, 0.0)
        sel_ref[row] = jnp.where((blk < i) & (rank < float(MOBA_TOPK)), 1.0, 0.0)

        def scores(blk):
            return _nt_dot(k_ref[row, pl.ds(pl.multiple_of(blk * TQ, TQ), TQ), :], qs)

        def step(s, blk, carry):
            return _softmax_step(s, vt_ref, _key_col(row, blk), *carry, keep=sel_ref[row, pl.ds(blk, 1), :] > 0.5)

        def diagonal(s, carry):
            return _softmax_step(jnp.where(key <= qry, s, MASK_VALUE), vt_ref, _key_col(row, i), *carry)

        return _Chain(sa_ref.at[row], sb_ref.at[row], scores, step, diagonal)

    chains = [chain(row) for row in range(ATTN_ROWS)]
    carries = _pipelined(i, chains, lambda n: n, [_softmax_init()] * ATTN_ROWS)
    for row, (_, acc) in enumerate(carries):
        _write_heads(o_ref, row, acc, gout_ref, True)


def _attn_call(body, name, q, k, vt, extra, scratch):
    b, _, qw = q.shape
    kw = k.shape[-1]
    rows = ATTN_ROWS
    assert b % rows == 0
    return pl.pallas_call(
        body, grid=(b // rows, SEQ // TQ),
        in_specs=[pl.BlockSpec((rows, TQ, qw), lambda g, i: (g, i, 0)),
                  pl.BlockSpec((rows, SEQ, kw), lambda g, i: (g, 0, 0)),
                  pl.BlockSpec((GROUP_WIDTH, rows * SEQ), lambda g, i: (0, g))]
                 + [_full(a.shape) for a in extra],
        out_specs=pl.BlockSpec((rows, TQ, GROUP_WIDTH), lambda g, i: (g, i, 0)),
        out_shape=jax.ShapeDtypeStruct((b, SEQ, GROUP_WIDTH), BF16),
        scratch_shapes=[pltpu.VMEM((rows, TQ, QW), F32), pltpu.VMEM((rows, TQ, QW), F32)] + scratch,
        compiler_params=_params("arbitrary", "arbitrary"), name=name,
    )(q, k, vt, *extra)


def _moba(q, k, vt, gout):
    scratch = [pltpu.VMEM((ATTN_ROWS, 16, 256), F32), pltpu.VMEM((ATTN_ROWS, 16, QW), F32)]
    return _attn_call(_moba_body, "moba", q, k, vt, [gout], scratch)


def _mla_body(q_ref, k_ref, vt_ref, gout_ref, o_ref, sa_ref, sb_ref):
    i = pl.program_id(1)
    key, qry = _key_query_iotas()

    def chain(row):
        qh = [q_ref[row, :, hd * LANES:(hd + 1) * LANES] for hd in range(N_HEADS)]

        def scores(blk):
            k0 = pl.multiple_of(blk * TQ, TQ)
            return jnp.concatenate(
                [_nt_dot(k_ref[row, pl.ds(k0, TQ), hd * LANES:(hd + 1) * LANES], qh[hd])
                 for hd in range(N_HEADS)], axis=1)

        def step(s, blk, carry):
            return _softmax_step(s, vt_ref, _key_col(row, blk), *carry)

        def diagonal(s, carry):
            return _softmax_step(jnp.where(key <= qry, s, MASK_VALUE), vt_ref, _key_col(row, i), *carry)

        return _Chain(sa_ref.at[row], sb_ref.at[row], scores, step, diagonal)

    chains = [chain(row) for row in range(ATTN_ROWS)]
    carries = _pipelined(i, chains, lambda n: n, [_softmax_init()] * ATTN_ROWS)
    for row, (_, acc) in enumerate(carries):
        _write_heads(o_ref, row, acc, gout_ref, True)


def _mla(q, k, vt, gout):
    return _attn_call(_mla_body, "mla", q, k, vt, [gout], [])


def _sb_body(q_ref, k_ref, vt_ref, uu_ref, gout_ref, o_ref, sa_ref, sb_ref):
    i = pl.program_id(1)
    key, qry = _key_query_iotas()
    strict = key < qry

    def neg_log_survive(z):
        return jnp.maximum(z, 0.0) + jnp.log(1.0 + jnp.exp2(jnp.abs(z) * (-LOG2E)))

    def suffix_sums(c):
        hi = c.astype(BF16)
        lo = (c - hi.astype(F32)).astype(BF16)
        t = jnp.dot(uu_ref[...], jnp.concatenate([hi, lo], axis=0), preferred_element_type=F32)
        return t[:TQ], t[TQ:TQ + 1]

    def chain(row):
        qs = _stack_heads(q_ref[row])

        def scores(blk):
            return _nt_dot(k_ref[row, pl.ds(pl.multiple_of(blk * TQ, TQ), TQ), :], qs)

        def step(z, blk, carry, diagonal=False):
            csum, acc = carry
            c = neg_log_survive(z)
            if diagonal:
                c = jnp.where(strict, c, 0.0)
            incl, total = suffix_sums(c)
            a = jnp.exp(z - (incl + csum))
            if diagonal:
                a = jnp.where(strict, a, 0.0)
            pv = _pv(vt_ref, _key_col(row, blk), a, False)
            return csum + total, [x + y for x, y in zip(acc, pv)]

        return _Chain(sa_ref.at[row], sb_ref.at[row], scores, step)

    chains = [chain(row) for row in range(ATTN_ROWS)]
    zero = (jnp.zeros((1, QW), F32), [jnp.zeros((HEAD_DIM, TQ), F32)] * N_HEADS)
    carries = [c.step(c.scores(i), i, zero, diagonal=True) for c in chains]
    carries = _pipelined(i, chains, lambda n: i - 1 - n, carries)
    for row, (_, acc) in enumerate(carries):
        _write_heads(o_ref, row, acc, gout_ref, False)


SB_SUM_ROWS = TQ + 16


def _sb(q, k, vt, gout):
    r = lax.broadcasted_iota(jnp.int32, (SB_SUM_ROWS, 2 * TQ), 0)
    c = lax.broadcasted_iota(jnp.int32, (SB_SUM_ROWS, 2 * TQ), 1) & (TQ - 1)
    uu = jnp.where((c >= r) | (r == TQ), 1.0, 0.0).astype(BF16)
    return _attn_call(_sb_body, "stickbreak", q, k, vt, [uu, gout], [])


def _ssd_body(z_ref, xbc_ref, dt_ref, cw_ref, cb_ref, dtb_ref, alog_ref, dskip_ref, ng_ref, o_ref,
              halo_ref, st_ref):
    c = pl.program_id(1)
    cl = SSM_CHUNK

    @pl.when(c == 0)
    def _():
        halo_ref[...] = jnp.zeros(halo_ref.shape, F32)
        st_ref[...] = jnp.zeros(st_ref.shape, F32)

    for row in range(SSD_ROWS):
        _ssd_chunk(row, z_ref, xbc_ref, dt_ref, cw_ref, cb_ref, dtb_ref, alog_ref, dskip_ref, ng_ref,
                   o_ref, halo_ref, st_ref)


def _ssd_chunk(row, z_ref, xbc_ref, dt_ref, cw_ref, cb_ref, dtb_ref, alog_ref, dskip_ref, ng_ref, o_ref,
               halo_ref, st_ref):
    cl = SSM_CHUNK
    xb = xbc_ref[row]
    ext = jnp.concatenate([halo_ref[row], xb], axis=0)
    u = cb_ref[...] + cw_ref[SSM_CONV - 1:SSM_CONV, :] * xb
    for tap in range(SSM_CONV - 1):
        shift = SSM_CONV - 1 - tap
        u = u + cw_ref[tap:tap + 1, :] * pltpu.roll(ext, shift, 0)[8:8 + cl]
    halo_ref[row] = xb[cl - 8:cl]
    xc = u * _sigmoid(u)
    xs = xc[:, 0:GROUP_WIDTH]
    bm = xc[:, GROUP_WIDTH:GROUP_WIDTH + LANES]
    cm = xc[:, GROUP_WIDTH + LANES:SSM_XBC]

    lane = lax.broadcasted_iota(jnp.int32, (cl, LANES), 1)
    head_lane = (lane >= DT_LANE) & (lane < DT_LANE + SSM_HEADS)
    dtv = dt_ref[row] + dtb_ref[...]
    dt = jnp.where(head_lane, jnp.maximum(dtv, 0.0) + _softplus_neg_abs(dtv), 0.0)
    a = dt * (-jnp.exp(alog_ref[...]))

    r = lax.broadcasted_iota(jnp.int32, (cl, cl), 0)
    s = lax.broadcasted_iota(jnp.int32, (cl, cl), 1)
    tril = s <= r
    tril_b = jnp.where(tril, 1.0, 0.0).astype(BF16)
    a1 = a.astype(BF16)
    a2 = (a - a1.astype(F32)).astype(BF16)
    a3 = (a - a1.astype(F32) - a2.astype(F32)).astype(BF16)
    acum = (jnp.dot(tril_b, a1, preferred_element_type=F32)
            + (jnp.dot(tril_b, a2, preferred_element_type=F32)
               + jnp.dot(tril_b, a3, preferred_element_type=F32)))
    acum_t = acum.T

    lane2 = lax.broadcasted_iota(jnp.int32, (cl, GROUP_WIDTH), 1)
    head2 = lane2 >> 6

    def per_head(v):
        out = jnp.broadcast_to(v[:, DT_LANE + 3:DT_LANE + 4], (cl, GROUP_WIDTH))
        for hd in (2, 1, 0):
            out = jnp.where(head2 == hd, v[:, DT_LANE + hd:DT_LANE + hd + 1], out)
        return out

    dt_full = per_head(dt)
    acum_full = per_head(acum)
    last_full = acum_full[cl - 1:cl, :]
    xdt = xs * dt_full
    xdt_b = xdt.astype(BF16)

    grp = lane >> 6
    bm_b = bm.astype(BF16)
    y = jnp.zeros((cl, GROUP_WIDTH), F32)
    for g in range(2):
        cg = jnp.where(grp == g, cm, 0.0).astype(BF16)
        gmat = _nt_dot(cg, bm_b)
        for hd in (2 * g, 2 * g + 1):
            seg = acum[:, DT_LANE + hd:DT_LANE + hd + 1] - acum_t[DT_LANE + hd:DT_LANE + hd + 1, :]
            lmat = jnp.where(tril, jnp.exp(jnp.where(tril, seg, 0.0)), 0.0)
            yd = jnp.dot((gmat * lmat).astype(BF16), xdt_b, preferred_element_type=F32)
            y = jnp.where(head2 == hd, yd, y)

    st_prev = st_ref[row]
    y_off = jnp.exp(acum_full) * jnp.dot(cm.astype(BF16), st_prev.astype(BF16), preferred_element_type=F32)
    dec = jnp.exp(last_full - acum_full)
    s_new = jnp.dot(bm.T.astype(BF16), (xdt * dec).astype(BF16), preferred_element_type=F32)
    rs = lax.broadcasted_iota(jnp.int32, (LANES, GROUP_WIDTH), 0) >> 6
    cs = lax.broadcasted_iota(jnp.int32, (LANES, GROUP_WIDTH), 1) >> 7
    st_ref[row] = jnp.exp(last_full) * st_prev + jnp.where(rs == cs, s_new, 0.0)

    y = y + y_off + dskip_ref[...] * xs
    zz = z_ref[row]
    y = y * (zz * _sigmoid(zz))
    ms = jnp.mean(y * y, axis=-1, keepdims=True)
    o_ref[row] = (y * lax.rsqrt(ms + EPS) * ng_ref[...]).astype(BF16)


SSD_ROWS = 4


def _ssd(z, xbc, dt, prm):
    b = z.shape[0]
    cl = SSM_CHUNK
    rows = SSD_ROWS
    assert b % rows == 0
    small = [prm["ssm_cw"], prm["ssm_cb"], prm["ssm_dtb"], prm["ssm_alog"], prm["ssm_d"], prm["ssm_ng"]]
    return pl.pallas_call(
        _ssd_body, grid=(b // rows, SEQ // cl),
        in_specs=[pl.BlockSpec((rows, cl, 256), lambda g, c: (g, c, 0)),
                  pl.BlockSpec((rows, cl, 512), lambda g, c: (g, c, 0)),
                  pl.BlockSpec((rows, cl, LANES), lambda g, c: (g, c, 0))]
                 + [_full(a.shape) for a in small],
        out_specs=pl.BlockSpec((rows, cl, 256), lambda g, c: (g, c, 0)),
        out_shape=jax.ShapeDtypeStruct((b, SEQ, 256), BF16),
        scratch_shapes=[pltpu.VMEM((rows, 8, SSM_XBC), F32), pltpu.VMEM((rows, LANES, GROUP_WIDTH), F32)],
        compiler_params=_params("arbitrary", "arbitrary"), name="ssd",
    )(z, xbc, dt, *small)


def _ffn_body(x_ref, o1_ref, o2_ref, o3_ref, o4_ref, xh_ref, oh1_ref, oh2_ref, oh3_ref, oh4_ref,
              wmix_ref, g_ref, win_ref, cw_ref, cb_ref, wout_ref, y_ref, act_ref):
    tm = TM_PROJ
    first = (pl.program_id(0) % (SEQ // tm)) == 0
    g = g_ref[...]

    def mix(xr, o_refs):
        acc = xr[...]
        for n, o_ref in enumerate(o_refs):
            acc = acc + jnp.dot(o_ref[...], wmix_ref[n * GROUP_WIDTH:(n + 1) * GROUP_WIDTH, :],
                                preferred_element_type=F32)
        return acc

    def norm(v):
        ms = jnp.mean(v * v, axis=-1, keepdims=True)
        return v * lax.rsqrt(ms + EPS) * g

    x = mix(x_ref, (o1_ref, o2_ref, o3_ref, o4_ref))
    x_halo = mix(xh_ref, (oh1_ref, oh2_ref, oh3_ref, oh4_ref))
    h_halo = jnp.where(first, 0.0, norm(x_halo)).astype(BF16)
    h_ext = jnp.concatenate([h_halo, norm(x).astype(BF16)], axis=0)

    def conv(u, lo, width):
        w = cw_ref[:, lo:lo + width]
        out = cb_ref[:, lo:lo + width] + w[2:3, :] * u[FFN_HALO:, :]
        out = out + w[1:2, :] * pltpu.roll(u, 1, 0)[FFN_HALO:, :]
        return out + w[0:1, :] * pltpu.roll(u, 2, 0)[FFN_HALO:, :]

    for lo, width in FFN_CHUNKS:
        gate = conv(jnp.dot(h_ext, win_ref[:, lo:lo + width], preferred_element_type=F32), lo, width)
        up = conv(jnp.dot(h_ext, win_ref[:, D_FF + lo:D_FF + lo + width], preferred_element_type=F32),
                  D_FF + lo, width)
        act_ref[:, lo:lo + width] = (gate * _sigmoid(gate) * up).astype(BF16)
    y_ref[...] = x + jnp.dot(act_ref[...], wout_ref[...], preferred_element_type=F32)


def _mix_ffn(x2d, outs, prm):
    t = x2d.shape[0]
    tm = TM_PROJ
    per = tm // FFN_HALO
    consts = [prm["w_out"], prm["ffn_g"], prm["ffn_win"], prm["ffn_cw"], prm["ffn_cb"], prm["ffn_wout"]]
    main = lambda w: pl.BlockSpec((tm, w), lambda i: (i, 0))
    halo = lambda w: pl.BlockSpec((FFN_HALO, w), lambda i: (jnp.maximum(i * per - 1, 0), 0))
    layer = prm["layer"]

    def const_spec(a):
        if a.ndim == 3:
            return pl.BlockSpec((None,) + a.shape[1:], lambda i: (layer, 0, 0))
        return _full(a.shape)

    return pl.pallas_call(
        _ffn_body, grid=(t // tm,),
        in_specs=[main(D_MODEL)] + [main(GROUP_WIDTH)] * 4 + [halo(D_MODEL)] + [halo(GROUP_WIDTH)] * 4
                 + [const_spec(a) for a in consts],
        out_specs=pl.BlockSpec((tm, D_MODEL), lambda i: (i, 0)),
        out_shape=jax.ShapeDtypeStruct((t, D_MODEL), F32),
        scratch_shapes=[pltpu.VMEM((tm, D_FF), BF16)],
        compiler_params=_params("arbitrary"), name="mix_ffn",
    )(x2d, *outs, x2d, *outs, *consts)


def _rope_tables(positions):
    pos = positions.reshape(-1).astype(F32)[:, None]
    inv_a = 1.0 / (ROPE_THETA ** (jnp.arange(0, HEAD_DIM, 2, dtype=F32) / HEAD_DIM))
    inv_m = 1.0 / (ROPE_THETA ** (jnp.arange(0, MLA_ROPE, 2, dtype=F32) / MLA_ROPE))
    ang = pos * jnp.concatenate([inv_a, inv_m])
    cos_all, sin_all = lax.optimization_barrier((jnp.cos(ang), jnp.sin(ang)))
    na, nm = HEAD_DIM // 2, MLA_ROPE // 2
    n_ang = na + nm
    sel = np.zeros((LANES, 4 * LANES), np.float32)
    for lane in range(LANES):
        sel[lane % na, lane] = 1.0
        sel[n_ang + lane % na, LANES + lane] = -1.0 if lane % HEAD_DIM < na else 1.0
        r = lane - MLA_NOPE
        if 0 <= r < MLA_ROPE:
            sel[na + r % nm, 2 * LANES + lane] = 1.0
            sel[n_ang + na + r % nm, 3 * LANES + lane] = -1.0 if r < nm else 1.0
        else:
            sel[2 * n_ang, 2 * LANES + lane] = 1.0
    n = pos.shape[0]
    feats = jnp.concatenate([cos_all, sin_all, jnp.ones_like(pos), jnp.zeros((n, LANES - 2 * n_ang - 1), F32)],
                            axis=1)
    return pl.pallas_call(
        _rope_expand_body, grid=(n // ROPE_TILE,),
        in_specs=[pl.BlockSpec((ROPE_TILE, LANES), lambda i: (i, 0)), _full(sel.shape)],
        out_specs=pl.BlockSpec((ROPE_TILE, 4 * LANES), lambda i: (i, 0)),
        out_shape=jax.ShapeDtypeStruct((n, 4 * LANES), F32),
        compiler_params=_params("arbitrary"), name="rope_tables",
    )(feats, jnp.asarray(sel, BF16))


ROPE_TILE = 2048


def _rope_expand_body(f_ref, sel_ref, o_ref):
    f = f_ref[...]
    f1 = f.astype(BF16)
    r1 = f - f1.astype(F32)
    f2 = r1.astype(BF16)
    f3 = (r1 - f2.astype(F32)).astype(BF16)
    sel = sel_ref[...]
    dot = lambda a: jnp.dot(a, sel, preferred_element_type=F32)
    o_ref[...] = dot(f1) + (dot(f2) + dot(f3))


def _pad_heads(w, per_head, width):
    lead = w.shape[:-1]
    w = w.reshape(*lead, N_HEADS, per_head)
    w = jnp.pad(w, [(0, 0)] * len(lead) + [(0, 0), (0, width - per_head)])
    return w.reshape(*lead, N_HEADS * width)


def _layer_params(l, mix_norm_g, w_in, moba_qk_g, mla_q_norm_g, mla_kv_norm_g, mla_w_uq, mla_w_ukv,
                  mla_qk_g, ssm_conv_w, ssm_conv_b, ssm_dt_bias, ssm_a_log, ssm_d, ssm_norm_g,
                  head_out_g, w_out, ffn_norm_g, ffn_w_in, ffn_conv_w, ffn_conv_b, ffn_w_out):
    w = w_in[l].astype(BF16)
    o = 0
    moba_w = w[:, o:o + 768]; o += 768
    cq_w = w[:, o:o + MLA_Q_RANK]; o += MLA_Q_RANK
    ckv_w = w[:, o:o + MLA_KV_RANK]; o += MLA_KV_RANK
    kr_w = w[:, o:o + MLA_ROPE]; o += MLA_ROPE
    sb_w = w[:, o:o + 768]; o += 768
    z_w = w[:, o:o + 256]; o += 256
    xbc_w = w[:, o:o + SSM_XBC]; o += SSM_XBC
    dt_w = w[:, o:o + SSM_HEADS]
    zc = lambda n: jnp.zeros((D_MODEL, n), BF16)
    half = HEAD_DIM // 2
    swap_a = (jnp.arange(GROUP_WIDTH) // HEAD_DIM) * HEAD_DIM + (jnp.arange(GROUP_WIDTH) % HEAD_DIM + half) % HEAD_DIM
    rh = MLA_ROPE // 2
    swap_r = jnp.concatenate([jnp.arange(rh, MLA_ROPE), jnp.arange(0, rh)])
    in_head = jnp.concatenate([jnp.arange(MLA_NOPE), MLA_NOPE + swap_r, jnp.arange(MLA_QK, LANES)])
    swap_m = (jnp.arange(N_HEADS)[:, None] * LANES + in_head[None, :]).reshape(-1)
    w_r = jnp.concatenate([
        moba_w[:, :512], moba_w[:, :256][:, swap_a], moba_w[:, 256:512][:, swap_a],
        cq_w, zc(256 - MLA_Q_RANK), ckv_w,
        kr_w[:, swap_r], zc(KR_LANE - MLA_ROPE), kr_w, dt_w, zc(LANES - DT_LANE - SSM_HEADS),
        sb_w[:, :512], z_w, xbc_w], axis=1)
    w_vt = jnp.stack([moba_w[:, 512:].T, sb_w[:, 512:].T])

    lane_row = lambda v, start: jnp.zeros((1, LANES), F32).at[0, start:start + v.shape[0]].set(v)
    moba_scale = jnp.array([[HEAD_DIM ** -0.5 * LOG2E], [1.0]], F32)
    mla_scale = jnp.array([[MLA_QK ** -0.5 * LOG2E], [1.0]], F32)
    gmoba = jnp.tile(moba_qk_g[l], (1, N_HEADS)) * moba_scale
    gmla = jnp.tile(jnp.pad(mla_qk_g[l], ((0, 0), (0, LANES - MLA_QK))), (1, N_HEADS)) * mla_scale
    wuq = jnp.pad(_pad_heads(mla_w_uq[l], MLA_QK, LANES), ((0, 256 - MLA_Q_RANK), (0, 0)))
    ukv = mla_w_ukv[l].reshape(MLA_KV_RANK, N_HEADS, MLA_NOPE + HEAD_DIM)
    hog = head_out_g[l].reshape(3, GROUP_WIDTH, 1)
    return {
        "mix_g": mix_norm_g[l][None, :], "w_in": w_r, "w_vt": w_vt,
        "gmoba": jnp.concatenate([gmoba, gmoba[:, swap_a]], axis=0),
        "gcq": jnp.pad(mla_q_norm_g[l], (0, 256 - MLA_Q_RANK))[None, :],
        "gckv": mla_kv_norm_g[l][None, :],
        "wuq": jnp.stack([wuq, wuq[:, swap_m]]).astype(BF16),
        "wuk": _pad_heads(ukv[:, :, :MLA_NOPE].reshape(MLA_KV_RANK, -1), MLA_NOPE, LANES).astype(BF16),
        "wuvt": ukv[:, :, MLA_NOPE:].reshape(MLA_KV_RANK, GROUP_WIDTH).T.astype(BF16),
        "gmla": jnp.concatenate([gmla, gmla[:, swap_m]], axis=0),
        "ssm_cw": ssm_conv_w[l], "ssm_cb": ssm_conv_b[l][None, :],
        "ssm_dtb": lane_row(ssm_dt_bias[l], DT_LANE), "ssm_alog": lane_row(ssm_a_log[l], DT_LANE),
        "ssm_d": jnp.repeat(ssm_d[l], HEAD_DIM)[None, :], "ssm_ng": ssm_norm_g[l][None, :],
        "g_moba_out": hog[0], "g_mla_out": hog[1], "g_sb_out": hog[2],
        "layer": l, "w_out": w_out, "ffn_win": ffn_w_in, "ffn_wout": ffn_w_out,
        "ffn_g": ffn_norm_g[l][None, :], "ffn_cw": ffn_conv_w[l], "ffn_cb": ffn_conv_b[l][None, :],
    }


def _layer(x2d, bsz, prm):
    (mq, mk, mvt, lq, lk, lvt, sq, sk, svt, z, xbc, dt) = _inproj(x2d, prm)
    r3 = lambda a: a.reshape(bsz, SEQ, a.shape[-1])
    o_moba = _moba(r3(mq), r3(mk), mvt, prm["g_moba_out"])
    o_mla = _mla(r3(lq), r3(lk), lvt, prm["g_mla_out"])
    o_sb = _sb(r3(sq), r3(sk), svt, prm["g_sb_out"])
    o_ssm = _ssd(r3(z), r3(xbc), r3(dt), prm)
    flat = lambda a: a.reshape(bsz * SEQ, GROUP_WIDTH)
    return _mix_ffn(x2d, [flat(o_moba), flat(o_mla), flat(o_sb), flat(o_ssm)], prm)


def kernel(x, positions, mix_norm_g, w_in, moba_qk_g, mla_q_norm_g, mla_kv_norm_g, mla_w_uq, mla_w_ukv,
           mla_qk_g, ssm_conv_w, ssm_conv_b, ssm_dt_bias, ssm_a_log, ssm_d, ssm_norm_g, head_out_g, w_out,
           ffn_norm_g, ffn_w_in, ffn_conv_w, ffn_conv_b, ffn_w_out):
    bsz, seq, d = x.shape
    assert seq == SEQ and d == D_MODEL
    rope = _rope_tables(positions)
    x2d = x.reshape(bsz * seq, d)
    w_out, ffn_w_in, ffn_w_out = (a.astype(BF16) for a in (w_out, ffn_w_in, ffn_w_out))
    for l in range(w_in.shape[0]):
        prm = _layer_params(l, mix_norm_g, w_in, moba_qk_g, mla_q_norm_g, mla_kv_norm_g, mla_w_uq,
                            mla_w_ukv, mla_qk_g, ssm_conv_w, ssm_conv_b, ssm_dt_bias, ssm_a_log, ssm_d,
                            ssm_norm_g, head_out_g, w_out, ffn_norm_g, ffn_w_in, ffn_conv_w, ffn_conv_b,
                            ffn_w_out)
        prm["rope"] = rope
        x2d = _layer(x2d, bsz, prm)
    return x2d.reshape(bsz, seq, d)
```

```python
import functools
import math

import numpy as np
import jax
import jax.numpy as jnp
from jax import lax
from jax.experimental import pallas as pl
from jax.experimental.pallas import tpu as pltpu

F32 = jnp.float32
BF16 = jnp.bfloat16

D_MODEL = 1024
SEQ = 2048
GROUP_WIDTH = 256
HEAD_DIM = 64
N_HEADS = 4
MOBA_BLOCK = 256
MOBA_TOPK = 3
MLA_Q_RANK = 192
MLA_KV_RANK = 128
MLA_NOPE = 64
MLA_ROPE = 32
MLA_QK = MLA_NOPE + MLA_ROPE
SSM_HEADS = 4
SSM_STATE = 64
SSM_CONV = 4
SSM_CHUNK = 128
SSM_XBC = 512
D_FF = 2816
FFN_CONV = 3
ROPE_THETA = 10000.0
EPS = 1e-6

LANES = 128
VMEM_LIMIT_BYTES = 56 * 1024 * 1024
MASK_VALUE = -1e30
LOG2E = math.log2(math.e)

C_MOBA = 0
C_CQ = 1024
C_CKV = 1280
C_KR = 1408
C_SB = 1536
C_Z = 2048
C_XBC = 2304
KR_LANE = 64
DT_LANE = 96

TM_PROJ = 1024
TM_INPROJ = 1024
TQ = 256
FFN_HALO = 16
FFN_CHUNKS = ((0, 512), (512, 512), (1024, 512), (1536, 512), (2048, 512), (2560, 256))


def _params(*sem):
    return pltpu.CompilerParams(dimension_semantics=sem, vmem_limit_bytes=VMEM_LIMIT_BYTES)


def _full(shape):
    n = len(shape)
    return pl.BlockSpec(shape, lambda *_: (0,) * n)


def _sigmoid(x):
    return 1.0 / (1.0 + jnp.exp(-x))


def _softplus_neg_abs(x):
    return jnp.log(1.0 + jnp.exp(-jnp.abs(x)))


def _group_sumsq(x, group):
    xx = x * x
    outs = []
    for c in range(x.shape[1] // LANES):
        slab = xx[:, c * LANES:(c + 1) * LANES]
        if group == LANES:
            s = jnp.sum(slab, axis=1, keepdims=True)
            outs.append(jnp.broadcast_to(s, slab.shape))
        else:
            lane = lax.broadcasted_iota(jnp.int32, slab.shape, 1)
            lo = lane < group
            s_lo = jnp.sum(jnp.where(lo, slab, 0.0), axis=1, keepdims=True)
            s_hi = jnp.sum(jnp.where(lo, 0.0, slab), axis=1, keepdims=True)
            outs.append(jnp.where(lo, s_lo, s_hi))
    return outs[0] if len(outs) == 1 else jnp.concatenate(outs, axis=1)


def _nt_dot(a, b):
    return lax.dot_general(a, b, (((1,), (1,)), ((), ())), preferred_element_type=F32)


def _inproj_body(x_ref, g_ref, w_ref, wvt_ref, cosa_ref, sina_ref, cosm_ref, sinm_ref, gmoba_ref, gcq_ref,
                 gckv_ref, wuq_ref, wuk_ref, wuvt_ref, gmla_ref,
                 mq_ref, mk_ref, mvt_ref, lq_ref, lk_ref, lvt_ref, sq_ref, sk_ref, svt_ref,
                 z_ref, xbc_ref, dt_ref):
    x = x_ref[...]
    ms = jnp.mean(x * x, axis=-1, keepdims=True)
    h = (x * lax.rsqrt(ms + EPS) * g_ref[...]).astype(BF16)

    def proj(lo, width):
        return jnp.dot(h, w_ref[:, lo:lo + width], preferred_element_type=F32)

    mvt_ref[...] = _nt_dot(wvt_ref[0], h).astype(BF16)
    svt_ref[...] = _nt_dot(wvt_ref[1], h).astype(BF16)

    def norm_rope(t, t_sw, group, n, gain_ref, idx, cos, sin):
        r = lax.rsqrt(_group_sumsq(t, group) * (1.0 / n) + EPS)
        return (r * ((t * gain_ref[idx:idx + 1, :]) * cos + (t_sw * gain_ref[idx + 2:idx + 3, :]) * sin)).astype(BF16)

    cosa = jnp.concatenate([cosa_ref[...]] * 2, axis=1)
    sina = jnp.concatenate([sina_ref[...]] * 2, axis=1)
    for idx, dst in ((0, mq_ref), (1, mk_ref)):
        t = proj(C_MOBA + idx * GROUP_WIDTH, GROUP_WIDTH)
        t_sw = proj(C_MOBA + (idx + 2) * GROUP_WIDTH, GROUP_WIDTH)
        dst[...] = norm_rope(t, t_sw, HEAD_DIM, HEAD_DIM, gmoba_ref, idx, cosa, sina)

    cq = proj(C_CQ, 256)
    cq = (cq * lax.rsqrt(jnp.sum(cq * cq, axis=-1, keepdims=True) * (1.0 / MLA_Q_RANK) + EPS)
          * gcq_ref[...]).astype(BF16)
    ckv = proj(C_CKV, LANES)
    ckv = (ckv * lax.rsqrt(jnp.mean(ckv * ckv, axis=-1, keepdims=True) + EPS) * gckv_ref[...]).astype(BF16)
    krdt = proj(C_KR, LANES)
    dt_ref[...] = krdt
    lane = lax.broadcasted_iota(jnp.int32, krdt.shape, 1)
    kr = jnp.where((lane >= KR_LANE) & (lane < KR_LANE + MLA_ROPE), krdt, 0.0)
    cosm = jnp.concatenate([cosm_ref[...]] * N_HEADS, axis=1)
    sinm = jnp.concatenate([sinm_ref[...]] * N_HEADS, axis=1)
    ql = jnp.dot(cq, wuq_ref[0], preferred_element_type=F32)
    ql_sw = jnp.dot(cq, wuq_ref[1], preferred_element_type=F32)
    kl = jnp.dot(ckv, wuk_ref[...], preferred_element_type=F32) + jnp.concatenate([kr] * N_HEADS, axis=1)
    kl_sw = jnp.concatenate([pltpu.roll(krdt, KR_LANE, 1)] * N_HEADS, axis=1)
    lq_ref[...] = norm_rope(ql, ql_sw, LANES, MLA_QK, gmla_ref, 0, cosm, sinm)
    lk_ref[...] = norm_rope(kl, kl_sw, LANES, MLA_QK, gmla_ref, 1, cosm, sinm)
    lvt_ref[...] = _nt_dot(wuvt_ref[...], ckv).astype(BF16)

    sq_ref[...] = (proj(C_SB, GROUP_WIDTH) * (HEAD_DIM ** -0.5)).astype(BF16)
    sk_ref[...] = proj(C_SB + GROUP_WIDTH, GROUP_WIDTH).astype(BF16)

    z_ref[...] = proj(C_Z, GROUP_WIDTH)
    xbc_ref[...] = proj(C_XBC, SSM_XBC)


def _inproj(x2d, prm):
    t = x2d.shape[0]
    tm = TM_INPROJ
    row = lambda w: pl.BlockSpec((tm, w), lambda i: (i, 0))
    tok = lambda w, dt: (jax.ShapeDtypeStruct((t, w), dt), row(w))
    tr = (jax.ShapeDtypeStruct((GROUP_WIDTH, t), BF16), pl.BlockSpec((GROUP_WIDTH, tm), lambda i: (0, i)))
    outs = [tok(256, BF16), tok(256, BF16), tr, tok(512, BF16), tok(512, BF16), tr,
            tok(256, BF16), tok(256, BF16), tr, tok(256, F32), tok(512, F32), tok(128, F32)]
    out_shapes = [o[0] for o in outs]
    out_specs = [o[1] for o in outs]
    consts = [prm["mix_g"], prm["w_in"], prm["w_vt"]]
    tables = [prm["rope"]] * 4
    table_specs = [pl.BlockSpec((tm, LANES), functools.partial(lambda j, i: (i, j), j)) for j in range(4)]
    tail = [prm["gmoba"], prm["gcq"], prm["gckv"], prm["wuq"], prm["wuk"], prm["wuvt"], prm["gmla"]]
    in_specs = ([row(D_MODEL)] + [_full(a.shape) for a in consts] + table_specs
                + [_full(a.shape) for a in tail])
    return pl.pallas_call(
        _inproj_body, grid=(t // tm,), in_specs=in_specs, out_specs=out_specs, out_shape=out_shapes,
        compiler_params=_params("arbitrary"), name="inproj",
    )(x2d, *consts, *tables, *tail)


QW = N_HEADS * TQ


def _head_slices(a):
    return [a[:, hd * TQ:(hd + 1) * TQ] for hd in range(N_HEADS)]


def _stack_heads(q):
    head_of_lane = lax.broadcasted_iota(jnp.int32, q.shape, 1) >> 6
    return jnp.concatenate([jnp.where(head_of_lane == hd, q, jnp.zeros_like(q)) for hd in range(N_HEADS)],
                           axis=0)


ONES_ROWS = 16


def _pv(vt_ref, k0, p, with_sum):
    ps = _head_slices(p.astype(BF16))
    outs = []
    for hd in range(N_HEADS):
        lhs = vt_ref[hd * HEAD_DIM:(hd + 1) * HEAD_DIM, pl.ds(k0, TQ)]
        if with_sum:
            r = lax.broadcasted_iota(jnp.int32, (ONES_ROWS, TQ), 0)
            lhs = jnp.concatenate([lhs, jnp.where(r == 0, 1.0, 0.0).astype(BF16)], axis=0)
        outs.append(jnp.dot(lhs, ps[hd], preferred_element_type=F32))
    return outs


def _key_query_iotas():
    key = lax.broadcasted_iota(jnp.int32, (TQ, QW), 0)
    qry = lax.broadcasted_iota(jnp.int32, (TQ, QW), 1) & (TQ - 1)
    return key, qry


def _softmax_init():
    return (jnp.full((1, QW), MASK_VALUE, F32), [jnp.zeros((HEAD_DIM + ONES_ROWS, TQ), F32)] * N_HEADS)


def _softmax_step(s, vt_ref, k0, m_run, acc, keep=None):
    m_new = jnp.maximum(m_run, jnp.max(s, axis=0, keepdims=True))
    m_sub = m_new
    if keep is not None:
        m_new = jnp.where(keep, m_new, m_run)
        m_sub = jnp.where(keep, m_new, -MASK_VALUE)
    al = _head_slices(jnp.exp2(m_run - m_new))
    pv = _pv(vt_ref, k0, jnp.exp2(s - m_sub), True)
    return m_new, [a * al[hd] + x for hd, (a, x) in enumerate(zip(acc, pv))]


ATTN_ROWS = 4


class _Chain:
    def __init__(self, s_a, s_b, scores, step, last=None):
        self.s_a, self.s_b, self.scores, self.step, self.last = s_a, s_b, scores, step, last


def _pipelined(i, chains, block_of, carries):
    def fetch(c, n):
        return c.scores(jnp.clip(block_of(n), 0, i))

    for c in chains:
        c.s_a[...] = fetch(c, 0)

    def pair(t, carries):
        n0 = 2 * t
        for c in chains:
            c.s_b[...] = fetch(c, n0 + 1)
        carries = [c.step(c.s_a[...], block_of(n0), cr) for c, cr in zip(chains, carries)]
        for c in chains:
            c.s_a[...] = fetch(c, n0 + 2)
        return [c.step(c.s_b[...], block_of(n0 + 1), cr) for c, cr in zip(chains, carries)]

    carries = lax.fori_loop(0, lax.shift_right_logical(i, 1), pair, carries)

    def tail_even(carries):
        return [cr if c.last is None else c.last(c.s_a[...], cr) for c, cr in zip(chains, carries)]

    def tail_odd(carries):
        for c in chains:
            if c.last is not None:
                c.s_b[...] = fetch(c, i)
        carries = [c.step(c.s_a[...], block_of(i - 1), cr) for c, cr in zip(chains, carries)]
        return [cr if c.last is None else c.last(c.s_b[...], cr) for c, cr in zip(chains, carries)]

    return lax.cond((i & 1) == 1, tail_odd, tail_even, carries)


def _write_heads(o_ref, row, acc, gain_ref, normalise):
    outs = []
    for hd in range(N_HEADS):
        o = acc[hd][:HEAD_DIM]
        if normalise:
            o = o * (1.0 / acc[hd][HEAD_DIM:HEAD_DIM + 1])
        ms = jnp.mean(o * o, axis=0, keepdims=True)
        outs.append(o * lax.rsqrt(ms + EPS) * gain_ref[hd * HEAD_DIM:(hd + 1) * HEAD_DIM, :])
    o_ref[row] = jnp.concatenate(outs, axis=0).T.astype(BF16)


def _key_col(row, blk):
    return pl.multiple_of(row * SEQ + blk * TQ, TQ)


def _moba_body(q_ref, k_ref, vt_ref, gout_ref, o_ref, sa_ref, sb_ref, kmean_ref, sel_ref):
    i = pl.program_id(1)
    nblk = SEQ // MOBA_BLOCK

    @pl.when(i == 0)
    def _():
        kmean_ref[...] = jnp.zeros(kmean_ref.shape, F32)
        for row in range(ATTN_ROWS):
            for n in range(nblk):
                kb = k_ref[row, n * MOBA_BLOCK:(n + 1) * MOBA_BLOCK, :].astype(F32)
                kmean_ref[row, n:n + 1, :] = jnp.mean(kb, axis=0, keepdims=True)

    key, qry = _key_query_iotas()

    def chain(row):
        qs = _stack_heads(q_ref[row])

        km = kmean_ref[row]
        km_hi = km.astype(BF16)
        km_lo = (km - km_hi.astype(F32)).astype(BF16)
        gate = _nt_dot(km_hi, qs) + _nt_dot(km_lo, qs)
        blk = lax.broadcasted_iota(jnp.int32, gate.shape, 0)
        rank = jnp.zeros(gate.shape, F32)
        for m in range(nblk - 1):
            gm = gate[m:m + 1, :]
            beats = (gm > gate) | ((gm == gate) & (blk > m))
            rank = rank + jnp.where(beats & (i > m), 1.0, 0.0)
        sel_ref[row] = jnp.where((blk < i) & (rank < float(MOBA_TOPK)), 1.0, 0.0)

        def scores(blk):
            return _nt_dot(k_ref[row, pl.ds(pl.multiple_of(blk * TQ, TQ), TQ), :], qs)

        def step(s, blk, carry):
            return _softmax_step(s, vt_ref, _key_col(row, blk), *carry, keep=sel_ref[row, pl.ds(blk, 1), :] > 0.5)

        def diagonal(s, carry):
            return _softmax_step(jnp.where(key <= qry, s, MASK_VALUE), vt_ref, _key_col(row, i), *carry)

        return _Chain(sa_ref.at[row], sb_ref.at[row], scores, step, diagonal)

    chains = [chain(row) for row in range(ATTN_ROWS)]
    carries = _pipelined(i, chains, lambda n: n, [_softmax_init()] * ATTN_ROWS)
    for row, (_, acc) in enumerate(carries):
        _write_heads(o_ref, row, acc, gout_ref, True)


def _attn_call(body, name, q, k, vt, extra, scratch):
    b, _, qw = q.shape
    kw = k.shape[-1]
    rows = ATTN_ROWS
    assert b % rows == 0
    return pl.pallas_call(
        body, grid=(b // rows, SEQ // TQ),
        in_specs=[pl.BlockSpec((rows, TQ, qw), lambda g, i: (g, i, 0)),
                  pl.BlockSpec((rows, SEQ, kw), lambda g, i: (g, 0, 0)),
                  pl.BlockSpec((GROUP_WIDTH, rows * SEQ), lambda g, i: (0, g))]
                 + [_full(a.shape) for a in extra],
        out_specs=pl.BlockSpec((rows, TQ, GROUP_WIDTH), lambda g, i: (g, i, 0)),
        out_shape=jax.ShapeDtypeStruct((b, SEQ, GROUP_WIDTH), BF16),
        scratch_shapes=[pltpu.VMEM((rows, TQ, QW), F32), pltpu.VMEM((rows, TQ, QW), F32)] + scratch,
        compiler_params=_params("arbitrary", "arbitrary"), name=name,
    )(q, k, vt, *extra)


def _moba(q, k, vt, gout):
    scratch = [pltpu.VMEM((ATTN_ROWS, 16, 256), F32), pltpu.VMEM((ATTN_ROWS, 16, QW), F32)]
    return _attn_call(_moba_body, "moba", q, k, vt, [gout], scratch)


def _mla_body(q_ref, k_ref, vt_ref, gout_ref, o_ref, sa_ref, sb_ref):
    i = pl.program_id(1)
    key, qry = _key_query_iotas()

    def chain(row):
        qh = [q_ref[row, :, hd * LANES:(hd + 1) * LANES] for hd in range(N_HEADS)]

        def scores(blk):
            k0 = pl.multiple_of(blk * TQ, TQ)
            return jnp.concatenate(
                [_nt_dot(k_ref[row, pl.ds(k0, TQ), hd * LANES:(hd + 1) * LANES], qh[hd])
                 for hd in range(N_HEADS)], axis=1)

        def step(s, blk, carry):
            return _softmax_step(s, vt_ref, _key_col(row, blk), *carry)

        def diagonal(s, carry):
            return _softmax_step(jnp.where(key <= qry, s, MASK_VALUE), vt_ref, _key_col(row, i), *carry)

        return _Chain(sa_ref.at[row], sb_ref.at[row], scores, step, diagonal)

    chains = [chain(row) for row in range(ATTN_ROWS)]
    carries = _pipelined(i, chains, lambda n: n, [_softmax_init()] * ATTN_ROWS)
    for row, (_, acc) in enumerate(carries):
        _write_heads(o_ref, row, acc, gout_ref, True)


def _mla(q, k, vt, gout):
    return _attn_call(_mla_body, "mla", q, k, vt, [gout], [])


def _sb_body(q_ref, k_ref, vt_ref, uu_ref, gout_ref, o_ref, sa_ref, sb_ref):
    i = pl.program_id(1)
    key, qry = _key_query_iotas()
    strict = key < qry

    def neg_log_survive(z):
        return jnp.maximum(z, 0.0) + jnp.log(1.0 + jnp.exp2(jnp.abs(z) * (-LOG2E)))

    def suffix_sums(c):
        hi = c.astype(BF16)
        lo = (c - hi.astype(F32)).astype(BF16)
        t = jnp.dot(uu_ref[...], jnp.concatenate([hi, lo], axis=0), preferred_element_type=F32)
        return t[:TQ], t[TQ:TQ + 1]

    def chain(row):
        qs = _stack_heads(q_ref[row])

        def scores(blk):
            return _nt_dot(k_ref[row, pl.ds(pl.multiple_of(blk * TQ, TQ), TQ), :], qs)

        def step(z, blk, carry, diagonal=False):
            csum, acc = carry
            c = neg_log_survive(z)
            if diagonal:
                c = jnp.where(strict, c, 0.0)
            incl, total = suffix_sums(c)
            a = jnp.exp(z - (incl + csum))
            if diagonal:
                a = jnp.where(strict, a, 0.0)
            pv = _pv(vt_ref, _key_col(row, blk), a, False)
            return csum + total, [x + y for x, y in zip(acc, pv)]

        return _Chain(sa_ref.at[row], sb_ref.at[row], scores, step)

    chains = [chain(row) for row in range(ATTN_ROWS)]
    zero = (jnp.zeros((1, QW), F32), [jnp.zeros((HEAD_DIM, TQ), F32)] * N_HEADS)
    carries = [c.step(c.scores(i), i, zero, diagonal=True) for c in chains]
    carries = _pipelined(i, chains, lambda n: i - 1 - n, carries)
    for row, (_, acc) in enumerate(carries):
        _write_heads(o_ref, row, acc, gout_ref, False)


SB_SUM_ROWS = TQ + 16


def _sb(q, k, vt, gout):
    r = lax.broadcasted_iota(jnp.int32, (SB_SUM_ROWS, 2 * TQ), 0)
    c = lax.broadcasted_iota(jnp.int32, (SB_SUM_ROWS, 2 * TQ), 1) & (TQ - 1)
    uu = jnp.where((c >= r) | (r == TQ), 1.0, 0.0).astype(BF16)
    return _attn_call(_sb_body, "stickbreak", q, k, vt, [uu, gout], [])


def _ssd_body(z_ref, xbc_ref, dt_ref, cw_ref, cb_ref, dtb_ref, alog_ref, dskip_ref, ng_ref, o_ref,
              halo_ref, st_ref):
    c = pl.program_id(1)
    cl = SSM_CHUNK

    @pl.when(c == 0)
    def _():
        halo_ref[...] = jnp.zeros(halo_ref.shape, F32)
        st_ref[...] = jnp.zeros(st_ref.shape, F32)

    for row in range(SSD_ROWS):
        _ssd_chunk(row, z_ref, xbc_ref, dt_ref, cw_ref, cb_ref, dtb_ref, alog_ref, dskip_ref, ng_ref,
                   o_ref, halo_ref, st_ref)


def _ssd_chunk(row, z_ref, xbc_ref, dt_ref, cw_ref, cb_ref, dtb_ref, alog_ref, dskip_ref, ng_ref, o_ref,
               halo_ref, st_ref):
    cl = SSM_CHUNK
    xb = xbc_ref[row]
    ext = jnp.concatenate([halo_ref[row], xb], axis=0)
    u = cb_ref[...] + cw_ref[SSM_CONV - 1:SSM_CONV, :] * xb
    for tap in range(SSM_CONV - 1):
        shift = SSM_CONV - 1 - tap
        u = u + cw_ref[tap:tap + 1, :] * pltpu.roll(ext, shift, 0)[8:8 + cl]
    halo_ref[row] = xb[cl - 8:cl]
    xc = u * _sigmoid(u)
    xs = xc[:, 0:GROUP_WIDTH]
    bm = xc[:, GROUP_WIDTH:GROUP_WIDTH + LANES]
    cm = xc[:, GROUP_WIDTH + LANES:SSM_XBC]

    lane = lax.broadcasted_iota(jnp.int32, (cl, LANES), 1)
    head_lane = (lane >= DT_LANE) & (lane < DT_LANE + SSM_HEADS)
    dtv = dt_ref[row] + dtb_ref[...]
    dt = jnp.where(head_lane, jnp.maximum(dtv, 0.0) + _softplus_neg_abs(dtv), 0.0)
    a = dt * (-jnp.exp(alog_ref[...]))

    r = lax.broadcasted_iota(jnp.int32, (cl, cl), 0)
    s = lax.broadcasted_iota(jnp.int32, (cl, cl), 1)
    tril = s <= r
    tril_b = jnp.where(tril, 1.0, 0.0).astype(BF16)
    a1 = a.astype(BF16)
    a2 = (a - a1.astype(F32)).astype(BF16)
    a3 = (a - a1.astype(F32) - a2.astype(F32)).astype(BF16)
    acum = (jnp.dot(tril_b, a1, preferred_element_type=F32)
            + (jnp.dot(tril_b, a2, preferred_element_type=F32)
               + jnp.dot(tril_b, a3, preferred_element_type=F32)))
    acum_t = acum.T

    lane2 = lax.broadcasted_iota(jnp.int32, (cl, GROUP_WIDTH), 1)
    head2 = lane2 >> 6

    def per_head(v):
        out = jnp.broadcast_to(v[:, DT_LANE + 3:DT_LANE + 4], (cl, GROUP_WIDTH))
        for hd in (2, 1, 0):
            out = jnp.where(head2 == hd, v[:, DT_LANE + hd:DT_LANE + hd + 1], out)
        return out

    dt_full = per_head(dt)
    acum_full = per_head(acum)
    last_full = acum_full[cl - 1:cl, :]
    xdt = xs * dt_full
    xdt_b = xdt.astype(BF16)

    grp = lane >> 6
    bm_b = bm.astype(BF16)
    y = jnp.zeros((cl, GROUP_WIDTH), F32)
    for g in range(2):
        cg = jnp.where(grp == g, cm, 0.0).astype(BF16)
        gmat = _nt_dot(cg, bm_b)
        for hd in (2 * g, 2 * g + 1):
            seg = acum[:, DT_LANE + hd:DT_LANE + hd + 1] - acum_t[DT_LANE + hd:DT_LANE + hd + 1, :]
            lmat = jnp.where(tril, jnp.exp(jnp.where(tril, seg, 0.0)), 0.0)
            yd = jnp.dot((gmat * lmat).astype(BF16), xdt_b, preferred_element_type=F32)
            y = jnp.where(head2 == hd, yd, y)

    st_prev = st_ref[row]
    y_off = jnp.exp(acum_full) * jnp.dot(cm.astype(BF16), st_prev.astype(BF16), preferred_element_type=F32)
    dec = jnp.exp(last_full - acum_full)
    s_new = jnp.dot(bm.T.astype(BF16), (xdt * dec).astype(BF16), preferred_element_type=F32)
    rs = lax.broadcasted_iota(jnp.int32, (LANES, GROUP_WIDTH), 0) >> 6
    cs = lax.broadcasted_iota(jnp.int32, (LANES, GROUP_WIDTH), 1) >> 7
    st_ref[row] = jnp.exp(last_full) * st_prev + jnp.where(rs == cs, s_new, 0.0)

    y = y + y_off + dskip_ref[...] * xs
    zz = z_ref[row]
    y = y * (zz * _sigmoid(zz))
    ms = jnp.mean(y * y, axis=-1, keepdims=True)
    o_ref[row] = (y * lax.rsqrt(ms + EPS) * ng_ref[...]).astype(BF16)


SSD_ROWS = 8


def _ssd(z, xbc, dt, prm):
    b = z.shape[0]
    cl = SSM_CHUNK
    rows = SSD_ROWS
    assert b % rows == 0
    small = [prm["ssm_cw"], prm["ssm_cb"], prm["ssm_dtb"], prm["ssm_alog"], prm["ssm_d"], prm["ssm_ng"]]
    return pl.pallas_call(
        _ssd_body, grid=(b // rows, SEQ // cl),
        in_specs=[pl.BlockSpec((rows, cl, 256), lambda g, c: (g, c, 0)),
                  pl.BlockSpec((rows, cl, 512), lambda g, c: (g, c, 0)),
                  pl.BlockSpec((rows, cl, LANES), lambda g, c: (g, c, 0))]
                 + [_full(a.shape) for a in small],
        out_specs=pl.BlockSpec((rows, cl, 256), lambda g, c: (g, c, 0)),
        out_shape=jax.ShapeDtypeStruct((b, SEQ, 256), BF16),
        scratch_shapes=[pltpu.VMEM((rows, 8, SSM_XBC), F32), pltpu.VMEM((rows, LANES, GROUP_WIDTH), F32)],
        compiler_params=_params("arbitrary", "arbitrary"), name="ssd",
    )(z, xbc, dt, *small)


def _ffn_body(x_ref, o1_ref, o2_ref, o3_ref, o4_ref, xh_ref, oh1_ref, oh2_ref, oh3_ref, oh4_ref,
              wmix_ref, g_ref, win_ref, cw_ref, cb_ref, wout_ref, y_ref, act_ref):
    tm = TM_PROJ
    first = (pl.program_id(0) % (SEQ // tm)) == 0
    g = g_ref[...]

    def mix(xr, o_refs):
        acc = xr[...]
        for n, o_ref in enumerate(o_refs):
            acc = acc + jnp.dot(o_ref[...], wmix_ref[n * GROUP_WIDTH:(n + 1) * GROUP_WIDTH, :],
                                preferred_element_type=F32)
        return acc

    def norm(v):
        ms = jnp.mean(v * v, axis=-1, keepdims=True)
        return v * lax.rsqrt(ms + EPS) * g

    x = mix(x_ref, (o1_ref, o2_ref, o3_ref, o4_ref))
    x_halo = mix(xh_ref, (oh1_ref, oh2_ref, oh3_ref, oh4_ref))
    h_halo = jnp.where(first, 0.0, norm(x_halo)).astype(BF16)
    h_ext = jnp.concatenate([h_halo, norm(x).astype(BF16)], axis=0)

    def conv(u, lo, width):
        w = cw_ref[:, lo:lo + width]
        out = cb_ref[:, lo:lo + width] + w[2:3, :] * u[FFN_HALO:, :]
        out = out + w[1:2, :] * pltpu.roll(u, 1, 0)[FFN_HALO:, :]
        return out + w[0:1, :] * pltpu.roll(u, 2, 0)[FFN_HALO:, :]

    for lo, width in FFN_CHUNKS:
        gate = conv(jnp.dot(h_ext, win_ref[:, lo:lo + width], preferred_element_type=F32), lo, width)
        up = conv(jnp.dot(h_ext, win_ref[:, D_FF + lo:D_FF + lo + width], preferred_element_type=F32),
                  D_FF + lo, width)
        act_ref[:, lo:lo + width] = (gate * _sigmoid(gate) * up).astype(BF16)
    y_ref[...] = x + jnp.dot(act_ref[...], wout_ref[...], preferred_element_type=F32)


def _mix_ffn(x2d, outs, prm):
    t = x2d.shape[0]
    tm = TM_PROJ
    per = tm // FFN_HALO
    consts = [prm["w_out"], prm["ffn_g"], prm["ffn_win"], prm["ffn_cw"], prm["ffn_cb"], prm["ffn_wout"]]
    main = lambda w: pl.BlockSpec((tm, w), lambda i: (i, 0))
    halo = lambda w: pl.BlockSpec((FFN_HALO, w), lambda i: (jnp.maximum(i * per - 1, 0), 0))
    layer = prm["layer"]

    def const_spec(a):
        if a.ndim == 3:
            return pl.BlockSpec((None,) + a.shape[1:], lambda i: (layer, 0, 0))
        return _full(a.shape)

    return pl.pallas_call(
        _ffn_body, grid=(t // tm,),
        in_specs=[main(D_MODEL)] + [main(GROUP_WIDTH)] * 4 + [halo(D_MODEL)] + [halo(GROUP_WIDTH)] * 4
                 + [const_spec(a) for a in consts],
        out_specs=pl.BlockSpec((tm, D_MODEL), lambda i: (i, 0)),
        out_shape=jax.ShapeDtypeStruct((t, D_MODEL), F32),
        scratch_shapes=[pltpu.VMEM((tm, D_FF), BF16)],
        compiler_params=_params("arbitrary"), name="mix_ffn",
    )(x2d, *outs, x2d, *outs, *consts)


def _rope_tables(positions):
    pos = positions.reshape(-1).astype(F32)[:, None]
    inv_a = 1.0 / (ROPE_THETA ** (jnp.arange(0, HEAD_DIM, 2, dtype=F32) / HEAD_DIM))
    inv_m = 1.0 / (ROPE_THETA ** (jnp.arange(0, MLA_ROPE, 2, dtype=F32) / MLA_ROPE))
    ang = pos * jnp.concatenate([inv_a, inv_m])
    cos_all, sin_all = lax.optimization_barrier((jnp.cos(ang), jnp.sin(ang)))
    na, nm = HEAD_DIM // 2, MLA_ROPE // 2
    n_ang = na + nm
    sel = np.zeros((LANES, 4 * LANES), np.float32)
    for lane in range(LANES):
        sel[lane % na, lane] = 1.0
        sel[n_ang + lane % na, LANES + lane] = -1.0 if lane % HEAD_DIM < na else 1.0
        r = lane - MLA_NOPE
        if 0 <= r < MLA_ROPE:
            sel[na + r % nm, 2 * LANES + lane] = 1.0
            sel[n_ang + na + r % nm, 3 * LANES + lane] = -1.0 if r < nm else 1.0
        else:
            sel[2 * n_ang, 2 * LANES + lane] = 1.0
    n = pos.shape[0]
    feats = jnp.concatenate([cos_all, sin_all, jnp.ones_like(pos), jnp.zeros((n, LANES - 2 * n_ang - 1), F32)],
                            axis=1)
    return pl.pallas_call(
        _rope_expand_body, grid=(n // ROPE_TILE,),
        in_specs=[pl.BlockSpec((ROPE_TILE, LANES), lambda i: (i, 0)), _full(sel.shape)],
        out_specs=pl.BlockSpec((ROPE_TILE, 4 * LANES), lambda i: (i, 0)),
        out_shape=jax.ShapeDtypeStruct((n, 4 * LANES), F32),
        compiler_params=_params("arbitrary"), name="rope_tables",
    )(feats, jnp.asarray(sel, BF16))


ROPE_TILE = 2048


def _rope_expand_body(f_ref, sel_ref, o_ref):
    f = f_ref[...]
    f1 = f.astype(BF16)
    r1 = f - f1.astype(F32)
    f2 = r1.astype(BF16)
    f3 = (r1 - f2.astype(F32)).astype(BF16)
    sel = sel_ref[...]
    dot = lambda a: jnp.dot(a, sel, preferred_element_type=F32)
    o_ref[...] = dot(f1) + (dot(f2) + dot(f3))


def _pad_heads(w, per_head, width):
    lead = w.shape[:-1]
    w = w.reshape(*lead, N_HEADS, per_head)
    w = jnp.pad(w, [(0, 0)] * len(lead) + [(0, 0), (0, width - per_head)])
    return w.reshape(*lead, N_HEADS * width)


def _layer_params(l, mix_norm_g, w_in, moba_qk_g, mla_q_norm_g, mla_kv_norm_g, mla_w_uq, mla_w_ukv,
                  mla_qk_g, ssm_conv_w, ssm_conv_b, ssm_dt_bias, ssm_a_log, ssm_d, ssm_norm_g,
                  head_out_g, w_out, ffn_norm_g, ffn_w_in, ffn_conv_w, ffn_conv_b, ffn_w_out):
    w = w_in[l].astype(BF16)
    o = 0
    moba_w = w[:, o:o + 768]; o += 768
    cq_w = w[:, o:o + MLA_Q_RANK]; o += MLA_Q_RANK
    ckv_w = w[:, o:o + MLA_KV_RANK]; o += MLA_KV_RANK
    kr_w = w[:, o:o + MLA_ROPE]; o += MLA_ROPE
    sb_w = w[:, o:o + 768]; o += 768
    z_w = w[:, o:o + 256]; o += 256
    xbc_w = w[:, o:o + SSM_XBC]; o += SSM_XBC
    dt_w = w[:, o:o + SSM_HEADS]
    zc = lambda n: jnp.zeros((D_MODEL, n), BF16)
    half = HEAD_DIM // 2
    swap_a = (jnp.arange(GROUP_WIDTH) // HEAD_DIM) * HEAD_DIM + (jnp.arange(GROUP_WIDTH) % HEAD_DIM + half) % HEAD_DIM
    rh = MLA_ROPE // 2
    swap_r = jnp.concatenate([jnp.arange(rh, MLA_ROPE), jnp.arange(0, rh)])
    in_head = jnp.concatenate([jnp.arange(MLA_NOPE), MLA_NOPE + swap_r, jnp.arange(MLA_QK, LANES)])
    swap_m = (jnp.arange(N_HEADS)[:, None] * LANES + in_head[None, :]).reshape(-1)
    w_r = jnp.concatenate([
        moba_w[:, :512], moba_w[:, :256][:, swap_a], moba_w[:, 256:512][:, swap_a],
        cq_w, zc(256 - MLA_Q_RANK), ckv_w,
        kr_w[:, swap_r], zc(KR_LANE - MLA_ROPE), kr_w, dt_w, zc(LANES - DT_LANE - SSM_HEADS),
        sb_w[:, :512], z_w, xbc_w], axis=1)
    w_vt = jnp.stack([moba_w[:, 512:].T, sb_w[:, 512:].T])

    lane_row = lambda v, start: jnp.zeros((1, LANES), F32).at[0, start:start + v.shape[0]].set(v)
    moba_scale = jnp.array([[HEAD_DIM ** -0.5 * LOG2E], [1.0]], F32)
    mla_scale = jnp.array([[MLA_QK ** -0.5 * LOG2E], [1.0]], F32)
    gmoba = jnp.tile(moba_qk_g[l], (1, N_HEADS)) * moba_scale
    gmla = jnp.tile(jnp.pad(mla_qk_g[l], ((0, 0), (0, LANES - MLA_QK))), (1, N_HEADS)) * mla_scale
    wuq = jnp.pad(_pad_heads(mla_w_uq[l], MLA_QK, LANES), ((0, 256 - MLA_Q_RANK), (0, 0)))
    ukv = mla_w_ukv[l].reshape(MLA_KV_RANK, N_HEADS, MLA_NOPE + HEAD_DIM)
    hog = head_out_g[l].reshape(3, GROUP_WIDTH, 1)
    return {
        "mix_g": mix_norm_g[l][None, :], "w_in": w_r, "w_vt": w_vt,
        "gmoba": jnp.concatenate([gmoba, gmoba[:, swap_a]], axis=0),
        "gcq": jnp.pad(mla_q_norm_g[l], (0, 256 - MLA_Q_RANK))[None, :],
        "gckv": mla_kv_norm_g[l][None, :],
        "wuq": jnp.stack([wuq, wuq[:, swap_m]]).astype(BF16),
        "wuk": _pad_heads(ukv[:, :, :MLA_NOPE].reshape(MLA_KV_RANK, -1), MLA_NOPE, LANES).astype(BF16),
        "wuvt": ukv[:, :, MLA_NOPE:].reshape(MLA_KV_RANK, GROUP_WIDTH).T.astype(BF16),
        "gmla": jnp.concatenate([gmla, gmla[:, swap_m]], axis=0),
        "ssm_cw": ssm_conv_w[l], "ssm_cb": ssm_conv_b[l][None, :],
        "ssm_dtb": lane_row(ssm_dt_bias[l], DT_LANE), "ssm_alog": lane_row(ssm_a_log[l], DT_LANE),
        "ssm_d": jnp.repeat(ssm_d[l], HEAD_DIM)[None, :], "ssm_ng": ssm_norm_g[l][None, :],
        "g_moba_out": hog[0], "g_mla_out": hog[1], "g_sb_out": hog[2],
        "layer": l, "w_out": w_out, "ffn_win": ffn_w_in, "ffn_wout": ffn_w_out,
        "ffn_g": ffn_norm_g[l][None, :], "ffn_cw": ffn_conv_w[l], "ffn_cb": ffn_conv_b[l][None, :],
    }


def _layer(x2d, bsz, prm):
    (mq, mk, mvt, lq, lk, lvt, sq, sk, svt, z, xbc, dt) = _inproj(x2d, prm)
    r3 = lambda a: a.reshape(bsz, SEQ, a.shape[-1])
    o_moba = _moba(r3(mq), r3(mk), mvt, prm["g_moba_out"])
    o_mla = _mla(r3(lq), r3(lk), lvt, prm["g_mla_out"])
    o_sb = _sb(r3(sq), r3(sk), svt, prm["g_sb_out"])
    o_ssm = _ssd(r3(z), r3(xbc), r3(dt), prm)
    flat = lambda a: a.reshape(bsz * SEQ, GROUP_WIDTH)
    return _mix_ffn(x2d, [flat(o_moba), flat(o_mla), flat(o_sb), flat(o_ssm)], prm)


def kernel(x, positions, mix_norm_g, w_in, moba_qk_g, mla_q_norm_g, mla_kv_norm_g, mla_w_uq, mla_w_ukv,
           mla_qk_g, ssm_conv_w, ssm_conv_b, ssm_dt_bias, ssm_a_log, ssm_d, ssm_norm_g, head_out_g, w_out,
           ffn_norm_g, ffn_w_in, ffn_conv_w, ffn_conv_b, ffn_w_out):
    bsz, seq, d = x.shape
    assert seq == SEQ and d == D_MODEL
    rope = _rope_tables(positions)
    x2d = x.reshape(bsz * seq, d)
    w_out, ffn_w_in, ffn_w_out = (a.astype(BF16) for a in (w_out, ffn_w_in, ffn_w_out))
    for l in range(w_in.shape[0]):
        prm = _layer_params(l, mix_norm_g, w_in, moba_qk_g, mla_q_norm_g, mla_kv_norm_g, mla_w_uq,
                            mla_w_ukv, mla_qk_g, ssm_conv_w, ssm_conv_b, ssm_dt_bias, ssm_a_log, ssm_d,
                            ssm_norm_g, head_out_g, w_out, ffn_norm_g, ffn_w_in, ffn_conv_w, ffn_conv_b,
                            ffn_w_out)
        prm["rope"] = rope
        x2d = _layer(x2d, bsz, prm)
    return x2d.reshape(bsz, seq, d)
```
